```python
import jax
import jax.numpy as jnp
from jax import lax
import numpy as np

D_MODEL = 2048
BATCH = 4
SEQ = 8192
DEPTH = 1

CHUNK = 64
RET_HEADS = 4
RET_DK = 256
RET_DV = 256
GLA_HEADS = 4
GLA_DK = 128
GLA_DV = 256
GLA_GATE_RANK = 16
GLA_GATE_TAU = 16.0
ROPE_BASE = 10000.0
D_FF = 5632
CONV_WIDTH = 3
LN_EPS = 1e-5
DEEPNORM_ALPHA = (2.0 * DEPTH) ** 0.25
DEEPNORM_BETA = (8.0 * DEPTH) ** -0.25

RET_QK = RET_HEADS * RET_DK
RET_V = RET_HEADS * RET_DV
GLA_QK = GLA_HEADS * GLA_DK
GLA_V = GLA_HEADS * GLA_DV
MIX_WIDTH = RET_V + GLA_V
IN_SPLIT_SIZES = (RET_QK, RET_QK, RET_V, RET_V, GLA_QK, GLA_QK, GLA_V, GLA_V, GLA_GATE_RANK)
IN_WIDTH = 2 * RET_QK + 2 * RET_V + 2 * GLA_QK + 2 * GLA_V + GLA_GATE_RANK

kernel_name = 'hybrid_retention_gla_convffn_deepnorm'


def _layer_norm(x, g, b):
    xf = x.astype(jnp.float32)
    mu = jnp.mean(xf, axis=-1, keepdims=True)
    var = jnp.mean(jnp.square(xf - mu), axis=-1, keepdims=True)
    y = (xf - mu) * lax.rsqrt(var + LN_EPS) * g.astype(jnp.float32) + b.astype(jnp.float32)
    return y.astype(x.dtype)


def _head_norm(o, g, center):
    if center:
        o = o - jnp.mean(o, axis=-1, keepdims=True)
    o = o * lax.rsqrt(jnp.mean(jnp.square(o), axis=-1, keepdims=True) + LN_EPS)
    B, S, H, Dv = o.shape
    return o.reshape(B, S, H * Dv) * g.astype(jnp.float32)


def _rotary(x, pos):
    half = x.shape[-1] // 2
    inv_freq = ROPE_BASE ** (-jnp.arange(half, dtype=jnp.float32) / half)
    ang = pos[:, None] * inv_freq[None, :]
    cos = jnp.cos(ang)[None, :, None, :]
    sin = jnp.sin(ang)[None, :, None, :]
    x1, x2 = x[..., :half], x[..., half:]
    return jnp.concatenate([x1 * cos - x2 * sin, x1 * sin + x2 * cos], axis=-1)


def _to_chunks(t):
    B, S, H, D = t.shape
    return t.reshape(B, S // CHUNK, CHUNK, H, D).transpose(0, 3, 1, 2, 4)


def _from_chunks(t):
    B, H, N, C, D = t.shape
    return t.transpose(0, 2, 3, 1, 4).reshape(B, N * C, H, D)


def _chunk_scan(q_in, k_in, v, decay):
    B, H, N, C, Dk = q_in.shape
    Dv = v.shape[-1]

    def step(state, inp):
        q_n, k_n, v_n, d_n = inp
        o_n = jnp.einsum('bhcd,bhde->bhce', q_n, state)
        state = state * d_n[..., None] + jnp.einsum('bhcd,bhce->bhde', k_n, v_n)
        return state, o_n

    xs = (jnp.moveaxis(q_in, 2, 0), jnp.moveaxis(k_in, 2, 0),
          jnp.moveaxis(v, 2, 0), jnp.moveaxis(decay, 2, 0))
    init = jnp.zeros((B, H, Dk, Dv), q_in.dtype)
    _, o = lax.scan(step, init, xs)
    return jnp.moveaxis(o, 0, 2)


def _retention(q, k, v):
    B, S, H, Dk = q.shape
    N = S // CHUNK
    log_gamma = jnp.log1p(-jnp.exp2(-5.0 - jnp.arange(H, dtype=jnp.float32)))
    idx = jnp.arange(CHUNK, dtype=jnp.float32)
    intra_decay = jnp.exp(log_gamma[:, None, None] * jnp.abs(idx[:, None] - idx[None, :]))
    qc = _to_chunks(q)
    kc = _to_chunks(k) * (Dk ** -0.5)
    vc = _to_chunks(v)
    scores = jnp.einsum('bhncd,bhnmd->bhncm', qc, kc) * intra_decay[None, :, None]
    o_intra = jnp.einsum('bhncm,bhnme->bhnce', scores, vc)
    q_dec = jnp.exp(log_gamma[:, None] * (idx + 1.0))[None, :, None, :, None]
    k_dec = jnp.exp(log_gamma[:, None] * (CHUNK - 1.0 - idx))[None, :, None, :, None]
    state_dec = jnp.broadcast_to(jnp.exp(log_gamma * CHUNK)[None, :, None, None], (B, H, N, Dk))
    o_cross = _chunk_scan(qc * q_dec, kc * k_dec, vc, state_dec)
    return _from_chunks(o_intra + o_cross)


def _gla(q, k, v, log_a):
    Dk = q.shape[-1]
    qc = _to_chunks(q) * (Dk ** -0.5)
    kc = _to_chunks(k)
    vc = _to_chunks(v)
    b = jnp.cumsum(_to_chunks(log_a), axis=3)
    bc = b - b[:, :, :, CHUNK // 2:CHUNK // 2 + 1]
    e_pos = jnp.exp(bc)
    e_neg = jnp.exp(-bc)
    a_causal = jnp.einsum('bhncd,bhnmd->bhncm', qc * e_pos, kc * e_neg)
    a_anti = jnp.einsum('bhncd,bhnmd->bhncm', qc * e_neg, kc * e_pos)
    pos = jnp.arange(CHUNK)
    causal = pos[:, None] >= pos[None, :]
    scores = jnp.where(causal, a_causal, a_anti)
    o_intra = jnp.einsum('bhncm,bhnme->bhnce', scores, vc)
    b_last = b[:, :, :, -1:]
    o_cross = _chunk_scan(qc * jnp.exp(b), kc * jnp.exp(b_last - b), vc, jnp.exp(b_last[:, :, :, 0]))
    return _from_chunks(o_intra + o_cross)


def _mixer(x, w_in, w_gla_gate, b_gla_gate, g_ret, g_gla, w_out):
    B, S, _ = x.shape
    h = (x @ w_in).astype(jnp.float32)
    parts = []
    start = 0
    for size in IN_SPLIT_SIZES:
        parts.append(h[..., start:start + size])
        start += size
    q_r, k_r, v_r, z_r, q_g, k_g, v_g, z_g, a_g = parts
    pos = jnp.arange(S, dtype=jnp.float32)
    q_r = _rotary(q_r.reshape(B, S, RET_HEADS, RET_DK), pos)
    k_r = _rotary(k_r.reshape(B, S, RET_HEADS, RET_DK), pos)
    o_r = _retention(q_r, k_r, v_r.reshape(B, S, RET_HEADS, RET_DV))
    o_r = _head_norm(o_r, g_ret, True) * jax.nn.silu(z_r)
    log_a = jax.nn.log_sigmoid(a_g @ w_gla_gate.astype(jnp.float32)
                               + b_gla_gate.astype(jnp.float32)) / GLA_GATE_TAU
    o_g = _gla(q_g.reshape(B, S, GLA_HEADS, GLA_DK), k_g.reshape(B, S, GLA_HEADS, GLA_DK),
               v_g.reshape(B, S, GLA_HEADS, GLA_DV), log_a.reshape(B, S, GLA_HEADS, GLA_DK))
    o_g = _head_norm(o_g, g_gla, False) * jax.nn.silu(z_g)
    o = jnp.concatenate([o_r, o_g], axis=-1).astype(x.dtype)
    return o @ w_out


def _conv_ffn(x, w_up, w_conv, b_conv, w_down):
    u = x @ w_up
    u = lax.conv_general_dilated(u, w_conv[:, None, :], window_strides=(1,),
                                 padding=[(CONV_WIDTH - 1, 0)],
                                 dimension_numbers=('NWC', 'WIO', 'NWC'),
                                 feature_group_count=u.shape[-1]) + b_conv
    val, gate = jnp.split(u, 2, axis=-1)
    return (jax.nn.silu(gate) * val) @ w_down


def setup_inputs(seed: int = 0) -> dict:
    key = jax.random.key(seed)
    ks = jax.random.split(key, 16)
    nrm = jax.random.normal
    f32 = jnp.float32
    return {
        'x': nrm(ks[0], (BATCH, SEQ, D_MODEL), f32),
        'w_in': nrm(ks[1], (DEPTH, D_MODEL, IN_WIDTH), f32) * D_MODEL ** -0.5,
        'w_gla_gate': nrm(ks[2], (DEPTH, GLA_GATE_RANK, GLA_QK), f32) * GLA_GATE_RANK ** -0.5,
        'b_gla_gate': 0.1 * nrm(ks[3], (DEPTH, GLA_QK), f32),
        'g_ret': 1.0 + 0.02 * nrm(ks[4], (DEPTH, RET_V), f32),
        'g_gla': 1.0 + 0.02 * nrm(ks[5], (DEPTH, GLA_V), f32),
        'w_out': nrm(ks[6], (DEPTH, MIX_WIDTH, D_MODEL), f32) * MIX_WIDTH ** -0.5 * DEEPNORM_BETA,
        'ln1_g': 1.0 + 0.02 * nrm(ks[7], (DEPTH, D_MODEL), f32),
        'ln1_b': 0.02 * nrm(ks[8], (DEPTH, D_MODEL), f32),
        'w_up': nrm(ks[9], (DEPTH, D_MODEL, 2 * D_FF), f32) * D_MODEL ** -0.5,
        'w_conv': nrm(ks[10], (DEPTH, CONV_WIDTH, 2 * D_FF), f32) * CONV_WIDTH ** -0.5,
        'b_conv': 0.02 * nrm(ks[11], (DEPTH, 2 * D_FF), f32),
        'w_down': nrm(ks[12], (DEPTH, D_FF, D_MODEL), f32) * D_FF ** -0.5 * DEEPNORM_BETA,
        'ln2_g': 1.0 + 0.02 * nrm(ks[13], (DEPTH, D_MODEL), f32),
        'ln2_b': 0.02 * nrm(ks[14], (DEPTH, D_MODEL), f32),
    }


def reference(x, w_in, w_gla_gate, b_gla_gate, g_ret, g_gla, w_out, ln1_g, ln1_b,
              w_up, w_conv, b_conv, w_down, ln2_g, ln2_b):
    for layer in range(DEPTH):
        mix = _mixer(x, w_in[layer], w_gla_gate[layer], b_gla_gate[layer],
                     g_ret[layer], g_gla[layer], w_out[layer])
        x = _layer_norm(DEEPNORM_ALPHA * x + mix, ln1_g[layer], ln1_b[layer])
        ffn = _conv_ffn(x, w_up[layer], w_conv[layer], b_conv[layer], w_down[layer])
        x = _layer_norm(DEEPNORM_ALPHA * x + ffn, ln2_g[layer], ln2_b[layer])
    return x
```

```python
import functools

import jax
import jax.numpy as jnp
from jax import lax
from jax.experimental import pallas as pl
from jax.experimental.pallas import tpu as pltpu

D_MODEL = 2048
CHUNK = 64
RET_HEADS = 4
RET_DK = 256
RET_DV = 256
GLA_HEADS = 4
GLA_DK = 128
GLA_DV = 256
GLA_GATE_RANK = 16
GLA_GATE_TAU = 16.0
ROPE_BASE = 10000.0
D_FF = 5632
CONV_WIDTH = 3
LN_EPS = 1e-5
DEPTH = 1
DEEPNORM_ALPHA = (2.0 * DEPTH) ** 0.25

RET_QK = RET_HEADS * RET_DK
RET_V = RET_HEADS * RET_DV
GLA_QK = GLA_HEADS * GLA_DK
GLA_V = GLA_HEADS * GLA_DV
H_WIDTH = 2 * RET_QK + 2 * RET_V + 2 * GLA_QK + 2 * GLA_V

LANES = 128
BF16_ROWS = 16
VMEM_LIMIT = 56 * 1024 * 1024

F32 = jnp.float32
BF16 = jnp.bfloat16


def _dot(a, b):
    return jnp.dot(a, b, preferred_element_type=F32)


def _dot_nt(a, b):
    return lax.dot_general(a, b, (((1,), (1,)), ((), ())), preferred_element_type=F32)


def _dot_tn(a, b):
    return lax.dot_general(a, b, (((0,), (0,)), ((), ())), preferred_element_type=F32)


def _silu(z):
    return z / (1.0 + jnp.exp(-z))


def _layer_norm(y, g, b):
    mu = jnp.mean(y, axis=-1, keepdims=True)
    yc = y - mu
    var = jnp.mean(yc * yc, axis=-1, keepdims=True)
    return yc * lax.rsqrt(var + LN_EPS) * g + b


IN_TM = 1024
IN_TN = 1024


def _in_proj_kernel(x_ref, w_ref, cos_ref, sin_ref, o_ref):
    j = pl.program_id(1)
    acc = _dot(x_ref[...], w_ref[...])

    def rotary(scale):
        cos = cos_ref[...]
        sin = sin_ref[...]
        for hd in range(IN_TN // RET_DK):
            lo = hd * RET_DK
            x1 = acc[:, lo:lo + LANES]
            x2 = acc[:, lo + LANES:lo + 2 * LANES]
            o_ref[:, lo:lo + LANES] = ((x1 * cos - x2 * sin) * scale).astype(o_ref.dtype)
            o_ref[:, lo + LANES:lo + 2 * LANES] = ((x1 * sin + x2 * cos) * scale).astype(o_ref.dtype)

    @pl.when(j == 0)
    def _():
        rotary(1.0)

    @pl.when(j == 1)
    def _():
        rotary(RET_DK ** -0.5)

    @pl.when(j == 4)
    def _():
        o_ref[:, :GLA_QK] = (acc[:, :GLA_QK] * (GLA_DK ** -0.5)).astype(o_ref.dtype)
        o_ref[:, GLA_QK:] = acc[:, GLA_QK:].astype(o_ref.dtype)

    @pl.when((j == 2) | (j == 3) | (j >= 5))
    def _():
        o_ref[...] = acc.astype(o_ref.dtype)


def _in_proj(xb, w_main, cos, sin, seq):
    m = xb.shape[0]
    nseq = seq // IN_TM
    return pl.pallas_call(
        _in_proj_kernel,
        out_shape=jax.ShapeDtypeStruct((m, H_WIDTH), BF16),
        grid=(m // IN_TM, H_WIDTH // IN_TN),
        in_specs=[
            pl.BlockSpec((IN_TM, D_MODEL), lambda i, j: (i, 0)),
            pl.BlockSpec((D_MODEL, IN_TN), lambda i, j: (0, j)),
            pl.BlockSpec((IN_TM, LANES), lambda i, j: (i % nseq, 0)),
            pl.BlockSpec((IN_TM, LANES), lambda i, j: (i % nseq, 0)),
        ],
        out_specs=pl.BlockSpec((IN_TM, IN_TN), lambda i, j: (i, j)),
        compiler_params=pltpu.CompilerParams(
            dimension_semantics=("parallel", "arbitrary"), vmem_limit_bytes=VMEM_LIMIT),
        name="in_proj",
    )(xb, w_main, cos, sin)


GATE_TM = 256


def _gate_kernel(x_ref, wa_ref, wg_ref, bias_ref, tri_ref, b_ref):
    a = _dot(x_ref[...], wa_ref[...])
    z = jnp.dot(a, wg_ref[...], preferred_element_type=F32,
                precision=lax.Precision.HIGHEST) + bias_ref[...]
    log_a = (jnp.minimum(z, 0.0) - jnp.log1p(jnp.exp(-jnp.abs(z)))) * (1.0 / GLA_GATE_TAU)
    hi = log_a.astype(BF16)
    r1 = log_a - hi.astype(F32)
    mid = r1.astype(BF16)
    lo = (r1 - mid.astype(F32)).astype(BF16)
    tri = tri_ref[...]
    b_ref[...] = _dot(tri, hi) + _dot(tri, mid) + _dot(tri, lo)


def _gate(xb, w_a, w_gate, bias, tri):
    m = xb.shape[0]
    return pl.pallas_call(
        _gate_kernel,
        out_shape=jax.ShapeDtypeStruct((m, GLA_QK), F32),
        grid=(m // GATE_TM,),
        in_specs=[
            pl.BlockSpec((GATE_TM, D_MODEL), lambda i: (i, 0)),
            pl.BlockSpec((D_MODEL, LANES), lambda i: (0, 0)),
            pl.BlockSpec((LANES, GLA_QK), lambda i: (0, 0)),
            pl.BlockSpec((1, GLA_QK), lambda i: (0, 0)),
            pl.BlockSpec((GATE_TM, GATE_TM), lambda i: (0, 0)),
        ],
        out_specs=pl.BlockSpec((GATE_TM, GLA_QK), lambda i: (i, 0)),
        compiler_params=pltpu.CompilerParams(
            dimension_semantics=("parallel",), vmem_limit_bytes=VMEM_LIMIT),
        name="gla_gate",
    )(xb, w_a, w_gate, bias, tri)


MIX_T = 512


def _mixer_kernel(qr_ref, kr_ref, vr_ref, zr_ref, qg_ref, kg_ref, vg_ref, zg_ref, bg_ref,
                  dmat_ref, qdec_ref, kdec_ref, sdec_ref, gr_ref, gg_ref,
                  or_ref, og_ref, sr_ref, sg_ref):
    @pl.when(pl.program_id(2) == 0)
    def _():
        sr_ref[...] = jnp.zeros_like(sr_ref)
        sg_ref[...] = jnp.zeros_like(sg_ref)

    dmat = dmat_ref[0]
    qdec = qdec_ref[0]
    kdec = kdec_ref[0]
    sdec = sdec_ref[0]
    g_ret = gr_ref[...]
    g_gla = gg_ref[...]
    row = lax.broadcasted_iota(jnp.int32, (CHUNK, CHUNK), 0)
    col = lax.broadcasted_iota(jnp.int32, (CHUNK, CHUNK), 1)
    causal = row >= col

    def chunk(c, carry):
        r0 = pl.multiple_of(c * CHUNK, CHUNK)
        rows = pl.ds(r0, CHUNK)

        q = qr_ref[rows, :]
        k = kr_ref[rows, :]
        v = vr_ref[rows, :]
        p = (_dot_nt(q, k) * dmat).astype(BF16)
        s_old = sr_ref[...]
        o = _dot(p, v) + qdec * _dot(q, s_old.astype(BF16))
        vk = (v.astype(F32) * kdec).astype(BF16)
        sr_ref[...] = s_old * sdec + _dot_tn(k, vk)
        o = o - jnp.mean(o, axis=-1, keepdims=True)
        o = o * lax.rsqrt(jnp.mean(o * o, axis=-1, keepdims=True) + LN_EPS)
        or_ref[rows, :] = (o * g_ret * _silu(zr_ref[rows, :].astype(F32))).astype(or_ref.dtype)

        b = bg_ref[rows, :]
        b_mid = b[CHUNK // 2:CHUNK // 2 + 1, :]
        b_last = b[CHUNK - 1:CHUNK, :]
        bc = b - b_mid
        e_pos = jnp.exp(bc)
        e_neg = jnp.exp(-bc)
        q = qg_ref[rows, :].astype(F32)
        k = kg_ref[rows, :].astype(F32)
        v = vg_ref[rows, :]
        a_causal = _dot_nt((q * e_pos).astype(BF16), (k * e_neg).astype(BF16))
        a_anti = _dot_nt((q * e_neg).astype(BF16), (k * e_pos).astype(BF16))
        p = jnp.where(causal, a_causal, a_anti).astype(BF16)
        st_old = sg_ref[...]
        o = _dot(p, v) + _dot_nt((q * jnp.exp(b)).astype(BF16), st_old.astype(BF16))
        kb = (k * jnp.exp(b_last - b)).astype(BF16)
        sg_ref[...] = st_old * jnp.exp(b_last) + _dot_tn(v, kb)
        o = o * lax.rsqrt(jnp.mean(o * o, axis=-1, keepdims=True) + LN_EPS)
        og_ref[rows, :] = (o * g_gla * _silu(zg_ref[rows, :].astype(F32))).astype(og_ref.dtype)
        return carry

    lax.fori_loop(0, MIX_T // CHUNK, chunk, 0)


def _mixer(h, b_cum, dmat, qdec, kdec, sdec, g_ret, g_gla, batch, seq):
    m = h.shape[0]
    nt = seq // MIX_T
    rd = RET_DV
    c_kr = RET_QK // rd
    c_vr = 2 * RET_QK // rd
    c_zr = (2 * RET_QK + RET_V) // rd
    g0 = 2 * RET_QK + 2 * RET_V
    c_qg = g0 // GLA_DK
    c_kg = (g0 + GLA_QK) // GLA_DK
    c_vg = (g0 + 2 * GLA_QK) // rd
    c_zg = (g0 + 2 * GLA_QK + GLA_V) // rd

    def rows(bi, hi, ti):
        return bi * nt + ti

    def hspec(width, first):
        return pl.BlockSpec((MIX_T, width), lambda bi, hi, ti: (rows(bi, hi, ti), first + hi))

    def head_table(shape):
        return pl.BlockSpec((1,) + shape, lambda bi, hi, ti: (hi, 0, 0))

    out_spec = pl.BlockSpec((MIX_T, rd), lambda bi, hi, ti: (rows(bi, hi, ti), hi))
    return pl.pallas_call(
        _mixer_kernel,
        out_shape=(jax.ShapeDtypeStruct((m, RET_V), BF16), jax.ShapeDtypeStruct((m, GLA_V), BF16)),
        grid=(batch, RET_HEADS, nt),
        in_specs=[
            hspec(rd, 0), hspec(rd, c_kr), hspec(rd, c_vr), hspec(rd, c_zr),
            hspec(GLA_DK, c_qg), hspec(GLA_DK, c_kg), hspec(rd, c_vg), hspec(rd, c_zg),
            pl.BlockSpec((MIX_T, GLA_DK), lambda bi, hi, ti: (rows(bi, hi, ti), hi)),
            head_table((CHUNK, CHUNK)), head_table((CHUNK, RET_DV)), head_table((CHUNK, RET_DV)),
            head_table((1, RET_DV)),
            pl.BlockSpec((1, rd), lambda bi, hi, ti: (0, hi)),
            pl.BlockSpec((1, rd), lambda bi, hi, ti: (0, hi)),
        ],
        out_specs=(out_spec, out_spec),
        scratch_shapes=[pltpu.VMEM((RET_DK, RET_DV), F32), pltpu.VMEM((GLA_DV, GLA_DK), F32)],
        compiler_params=pltpu.CompilerParams(
            dimension_semantics=("parallel", "parallel", "arbitrary"), vmem_limit_bytes=VMEM_LIMIT),
        name="mixer",
    )(h, h, h, h, h, h, h, h, b_cum, dmat, qdec, kdec, sdec, g_ret, g_gla)


OUT_TM = 256


def _out_proj_kernel(or_ref, og_ref, w_ref, x_ref, g_ref, b_ref, y_ref, yb_ref):
    acc = _dot(or_ref[...], w_ref[:RET_V, :]) + _dot(og_ref[...], w_ref[RET_V:, :])
    y = _layer_norm(DEEPNORM_ALPHA * x_ref[...] + acc, g_ref[...], b_ref[...])
    y_ref[...] = y
    yb_ref[...] = y.astype(yb_ref.dtype)


def _out_proj(o_r, o_g, w_out, x2, g, b):
    m = x2.shape[0]
    row = lambda i: (i, 0)
    const = lambda i: (0, 0)
    return pl.pallas_call(
        _out_proj_kernel,
        out_shape=(jax.ShapeDtypeStruct((m, D_MODEL), F32), jax.ShapeDtypeStruct((m, D_MODEL), BF16)),
        grid=(m // OUT_TM,),
        in_specs=[
            pl.BlockSpec((OUT_TM, RET_V), row),
            pl.BlockSpec((OUT_TM, GLA_V), row),
            pl.BlockSpec((RET_V + GLA_V, D_MODEL), const),
            pl.BlockSpec((OUT_TM, D_MODEL), row),
            pl.BlockSpec((1, D_MODEL), const),
            pl.BlockSpec((1, D_MODEL), const),
        ],
        out_specs=(pl.BlockSpec((OUT_TM, D_MODEL), row), pl.BlockSpec((OUT_TM, D_MODEL), row)),
        compiler_params=pltpu.CompilerParams(
            dimension_semantics=("parallel",), vmem_limit_bytes=VMEM_LIMIT),
        name="out_proj_ln",
    )(o_r, o_g, w_out, x2, g, b)


UP_TM = 1024
UP_TF = 512
HALO = BF16_ROWS


def _up_conv_kernel(x_ref, halo_ref, wv_ref, wg_ref, cv_ref, cg_ref, bv_ref, bg_ref,
                    o_ref, lhs_ref, uv_ref, ug_ref, *, tiles_per_seq):
    i = pl.program_id(0)
    j = pl.program_id(1)

    @pl.when(j == 0)
    def _():
        keep = (i % tiles_per_seq != 0).astype(halo_ref.dtype)
        lhs_ref[:HALO, :] = halo_ref[...] * keep
        lhs_ref[HALO:, :] = x_ref[...]

    lhs = lhs_ref[...]
    uv_ref[...] = _dot(lhs, wv_ref[...])
    ug_ref[...] = _dot(lhs, wg_ref[...])

    def conv(u_ref, c_ref, bias_ref):
        out = bias_ref[...]
        for tap in range(CONV_WIDTH):
            start = HALO - (CONV_WIDTH - 1) + tap
            out = out + c_ref[tap:tap + 1, :] * u_ref[start:start + UP_TM, :]
        return out

    val = conv(uv_ref, cv_ref, bv_ref)
    gate = conv(ug_ref, cg_ref, bg_ref)
    o_ref[...] = (_silu(gate) * val).astype(o_ref.dtype)


def _up_conv(x1b, w_up, w_conv, b_conv, seq):
    m = x1b.shape[0]
    nf = D_FF // UP_TF
    tiles_per_seq = seq // UP_TM
    halo_blocks = UP_TM // HALO
    kern = functools.partial(_up_conv_kernel, tiles_per_seq=tiles_per_seq)
    return pl.pallas_call(
        kern,
        out_shape=jax.ShapeDtypeStruct((m, D_FF), BF16),
        grid=(m // UP_TM, nf),
        in_specs=[
            pl.BlockSpec((UP_TM, D_MODEL), lambda i, j: (i, 0)),
            pl.BlockSpec((HALO, D_MODEL), lambda i, j: (jnp.maximum(i * halo_blocks - 1, 0), 0)),
            pl.BlockSpec((D_MODEL, UP_TF), lambda i, j: (0, j)),
            pl.BlockSpec((D_MODEL, UP_TF), lambda i, j: (0, nf + j)),
            pl.BlockSpec((CONV_WIDTH, UP_TF), lambda i, j: (0, j)),
            pl.BlockSpec((CONV_WIDTH, UP_TF), lambda i, j: (0, nf + j)),
            pl.BlockSpec((1, UP_TF), lambda i, j: (0, j)),
            pl.BlockSpec((1, UP_TF), lambda i, j: (0, nf + j)),
        ],
        out_specs=pl.BlockSpec((UP_TM, UP_TF), lambda i, j: (i, j)),
        scratch_shapes=[
            pltpu.VMEM((UP_TM + HALO, D_MODEL), BF16),
            pltpu.VMEM((UP_TM + HALO, UP_TF), F32),
            pltpu.VMEM((UP_TM + HALO, UP_TF), F32),
        ],
        compiler_params=pltpu.CompilerParams(
            dimension_semantics=("parallel", "arbitrary"), vmem_limit_bytes=VMEM_LIMIT),
        name="up_conv_gate",
    )(x1b, x1b, w_up, w_up, w_conv, w_conv, b_conv, b_conv)


DOWN_TM = 256


def _down_proj_kernel(a_ref, w_ref, x_ref, g_ref, b_ref, y_ref):
    acc = _dot(a_ref[...], w_ref[...])
    y_ref[...] = _layer_norm(DEEPNORM_ALPHA * x_ref[...] + acc, g_ref[...], b_ref[...])


def _down_proj(act, w_down, x1, g, b):
    m = x1.shape[0]
    row = lambda i: (i, 0)
    const = lambda i: (0, 0)
    return pl.pallas_call(
        _down_proj_kernel,
        out_shape=jax.ShapeDtypeStruct((m, D_MODEL), F32),
        grid=(m // DOWN_TM,),
        in_specs=[
            pl.BlockSpec((DOWN_TM, D_FF), row),
            pl.BlockSpec((D_FF, D_MODEL), const, pipeline_mode=pl.Buffered(1)),
            pl.BlockSpec((DOWN_TM, D_MODEL), row),
            pl.BlockSpec((1, D_MODEL), const),
            pl.BlockSpec((1, D_MODEL), const),
        ],
        out_specs=pl.BlockSpec((DOWN_TM, D_MODEL), row),
        compiler_params=pltpu.CompilerParams(
            dimension_semantics=("parallel",), vmem_limit_bytes=VMEM_LIMIT),
        name="down_proj_ln",
    )(act, w_down, x1, g, b)


def _rotary_tables(seq):
    half = RET_DK // 2
    inv_freq = ROPE_BASE ** (-jnp.arange(half, dtype=F32) / half)
    ang = jnp.arange(seq, dtype=F32)[:, None] * inv_freq[None, :]
    return jnp.cos(ang), jnp.sin(ang)


def _retention_tables():
    log_gamma = jnp.log1p(-jnp.exp2(-5.0 - jnp.arange(RET_HEADS, dtype=F32)))
    idx = jnp.arange(CHUNK, dtype=F32)
    dmat = jnp.exp(log_gamma[:, None, None] * jnp.abs(idx[:, None] - idx[None, :]))
    qdec = jnp.exp(log_gamma[:, None] * (idx + 1.0))
    kdec = jnp.exp(log_gamma[:, None] * (CHUNK - 1.0 - idx))
    sdec = jnp.exp(log_gamma * CHUNK)
    qdec = jnp.broadcast_to(qdec[:, :, None], (RET_HEADS, CHUNK, RET_DV))
    kdec = jnp.broadcast_to(kdec[:, :, None], (RET_HEADS, CHUNK, RET_DV))
    sdec = jnp.broadcast_to(sdec[:, None, None], (RET_HEADS, 1, RET_DV))
    return dmat, qdec, kdec, sdec


def _chunk_tri(n):
    r = jnp.arange(n)
    same = (r[:, None] // CHUNK) == (r[None, :] // CHUNK)
    return (same & (r[:, None] >= r[None, :])).astype(BF16)


def kernel(x, w_in, w_gla_gate, b_gla_gate, g_ret, g_gla, w_out, ln1_g, ln1_b, w_up, w_conv,
           b_conv, w_down, ln2_g, ln2_b):
    batch, seq, d = x.shape
    m = batch * seq
    cos, sin = _rotary_tables(seq)
    dmat, qdec, kdec, sdec = _retention_tables()
    tri = _chunk_tri(GATE_TM)
    x2 = x.reshape(m, d)
    for layer in range(DEPTH):
        xb = x2.astype(BF16)
        w_in_l = w_in[layer]
        w_main = w_in_l[:, :H_WIDTH].astype(BF16)
        w_a = jnp.pad(w_in_l[:, H_WIDTH:], ((0, 0), (0, LANES - GLA_GATE_RANK))).astype(BF16)
        w_gate = jnp.pad(w_gla_gate[layer].astype(F32), ((0, LANES - GLA_GATE_RANK), (0, 0)))
        h = _in_proj(xb, w_main, cos, sin, seq)
        b_cum = _gate(xb, w_a, w_gate, b_gla_gate[layer].astype(F32)[None, :], tri)
        o_r, o_g = _mixer(h, b_cum, dmat, qdec, kdec, sdec,
                          g_ret[layer].astype(F32)[None, :], g_gla[layer].astype(F32)[None, :],
                          batch, seq)
        x1, x1b = _out_proj(o_r, o_g, w_out[layer].astype(BF16), x2,
                            ln1_g[layer][None, :], ln1_b[layer][None, :])
        act = _up_conv(x1b, w_up[layer].astype(BF16), w_conv[layer], b_conv[layer][None, :], seq)
        x2 = _down_proj(act, w_down[layer].astype(BF16), x1,
                        ln2_g[layer][None, :], ln2_b[layer][None, :])
    return x2.reshape(batch, seq, d)
```

```python
import functools

import jax
import jax.numpy as jnp
from jax import lax
from jax.experimental import pallas as pl
from jax.experimental.pallas import tpu as pltpu

D_MODEL = 2048
CHUNK = 64
RET_HEADS = 4
RET_DK = 256
RET_DV = 256
GLA_HEADS = 4
GLA_DK = 128
GLA_DV = 256
GLA_GATE_RANK = 16
GLA_GATE_TAU = 16.0
ROPE_BASE = 10000.0
D_FF = 5632
CONV_WIDTH = 3
LN_EPS = 1e-5
DEPTH = 1
DEEPNORM_ALPHA = (2.0 * DEPTH) ** 0.25

RET_QK = RET_HEADS * RET_DK
RET_V = RET_HEADS * RET_DV
GLA_QK = GLA_HEADS * GLA_DK
GLA_V = GLA_HEADS * GLA_DV
H_WIDTH = 2 * RET_QK + 2 * RET_V + 2 * GLA_QK + 2 * GLA_V

LANES = 128
VMEM_LIMIT = 56 * 1024 * 1024

F32 = jnp.float32
BF16 = jnp.bfloat16


def _dot(a, b):
    return jnp.dot(a, b, preferred_element_type=F32)


def _dot_nt(a, b):
    return lax.dot_general(a, b, (((1,), (1,)), ((), ())), preferred_element_type=F32)


def _dot_tn(a, b):
    return lax.dot_general(a, b, (((0,), (0,)), ((), ())), preferred_element_type=F32)


def _silu(z):
    return z / (1.0 + jnp.exp(-z))


def _layer_norm(y, g, b):
    mu = jnp.mean(y, axis=-1, keepdims=True)
    yc = y - mu
    var = jnp.mean(yc * yc, axis=-1, keepdims=True)
    return yc * lax.rsqrt(var + LN_EPS) * g + b


IN_TM = 1024
IN_TN = 1024
TRI = 256


def _in_proj_kernel(x_ref, w_ref, cos_ref, sin_ref, wa_ref, wg_ref, bias_ref, tri_ref,
                    o_ref, b_ref, xb_ref):
    j = pl.program_id(1)

    @pl.when(j == 0)
    def _():
        xb_ref[...] = x_ref[...].astype(xb_ref.dtype)
        a = _dot(xb_ref[...], wa_ref[...])
        z = _dot(a.astype(BF16), wg_ref[...]) + bias_ref[...]
        log_a = (jnp.minimum(z, 0.0) - jnp.log1p(jnp.exp(-jnp.abs(z)))) * (1.0 / GLA_GATE_TAU)
        hi = log_a.astype(BF16)
        lo = (log_a - hi.astype(F32)).astype(BF16)
        tri = tri_ref[...]
        for r in range(IN_TM // TRI):
            rows = slice(r * TRI, (r + 1) * TRI)
            b_ref[rows, :] = _dot(tri, hi[rows, :]) + _dot(tri, lo[rows, :])

    acc = _dot(xb_ref[...], w_ref[...])

    def rotary(scale):
        cos = cos_ref[...]
        sin = sin_ref[...]
        for hd in range(IN_TN // RET_DK):
            lo = hd * RET_DK
            x1 = acc[:, lo:lo + LANES]
            x2 = acc[:, lo + LANES:lo + 2 * LANES]
            o_ref[:, lo:lo + LANES] = ((x1 * cos - x2 * sin) * scale).astype(o_ref.dtype)
            o_ref[:, lo + LANES:lo + 2 * LANES] = ((x1 * sin + x2 * cos) * scale).astype(o_ref.dtype)

    @pl.when(j == 0)
    def _():
        rotary(1.0)

    @pl.when(j == 1)
    def _():
        rotary(RET_DK ** -0.5)

    @pl.when(j == 4)
    def _():
        o_ref[:, :GLA_QK] = (acc[:, :GLA_QK] * (GLA_DK ** -0.5)).astype(o_ref.dtype)
        o_ref[:, GLA_QK:] = acc[:, GLA_QK:].astype(o_ref.dtype)

    @pl.when((j == 2) | (j == 3) | (j >= 5))
    def _():
        o_ref[...] = acc.astype(o_ref.dtype)


def _in_proj(x2, w_main, cos, sin, w_a, w_gate, bias, tri, seq):
    m = x2.shape[0]
    nseq = seq // IN_TM
    const = lambda i, j: (0, 0)
    return pl.pallas_call(
        _in_proj_kernel,
        out_shape=(jax.ShapeDtypeStruct((m, H_WIDTH), BF16), jax.ShapeDtypeStruct((m, GLA_QK), F32)),
        grid=(m // IN_TM, H_WIDTH // IN_TN),
        in_specs=[
            pl.BlockSpec((IN_TM, D_MODEL), lambda i, j: (i, 0)),
            pl.BlockSpec((D_MODEL, IN_TN), lambda i, j: (0, j)),
            pl.BlockSpec((IN_TM, LANES), lambda i, j: (i % nseq, 0)),
            pl.BlockSpec((IN_TM, LANES), lambda i, j: (i % nseq, 0)),
            pl.BlockSpec((D_MODEL, LANES), const),
            pl.BlockSpec((LANES, GLA_QK), const),
            pl.BlockSpec((1, GLA_QK), const),
            pl.BlockSpec((TRI, TRI), const),
        ],
        out_specs=(pl.BlockSpec((IN_TM, IN_TN), lambda i, j: (i, j)),
                   pl.BlockSpec((IN_TM, GLA_QK), lambda i, j: (i, 0))),
        scratch_shapes=[pltpu.VMEM((IN_TM, D_MODEL), BF16)],
        compiler_params=pltpu.CompilerParams(
            dimension_semantics=("parallel", "arbitrary"), vmem_limit_bytes=VMEM_LIMIT),
        name="in_proj",
    )(x2, w_main, cos, sin, w_a, w_gate, bias, tri)


MIX_T = 512
MIX_UNROLL = 8


def _mixer_kernel(qr_ref, kr_ref, vr_ref, zr_ref, qg_ref, kg_ref, vg_ref, zg_ref, bg_ref,
                  dmat_ref, qdec_ref, kdec_ref, sdec_ref, gr_ref, gg_ref,
                  or_ref, og_ref, sr_ref, sg_ref):
    @pl.when(pl.program_id(2) == 0)
    def _():
        sr_ref[...] = jnp.zeros_like(sr_ref)
        sg_ref[...] = jnp.zeros_like(sg_ref)

    dmat = dmat_ref[0]
    qdec = qdec_ref[0]
    kdec = kdec_ref[0]
    sdec = sdec_ref[0]
    g_ret = gr_ref[...]
    g_gla = gg_ref[...]
    row = lax.broadcasted_iota(jnp.int32, (CHUNK, CHUNK), 0)
    col = lax.broadcasted_iota(jnp.int32, (CHUNK, CHUNK), 1)
    causal = row >= col

    def chunk(c, carry):
        r0 = pl.multiple_of(c * CHUNK, CHUNK)
        rows = pl.ds(r0, CHUNK)

        q = qr_ref[rows, :]
        k = kr_ref[rows, :]
        v = vr_ref[rows, :]
        p = (_dot_nt(q, k) * dmat).astype(BF16)
        s_old = sr_ref[...]
        o = _dot(p, v) + qdec * _dot(q, s_old.astype(BF16))
        vk = (v.astype(F32) * kdec).astype(BF16)
        sr_ref[...] = s_old * sdec + _dot_tn(k, vk)
        o = o - jnp.mean(o, axis=-1, keepdims=True)
        o = o * lax.rsqrt(jnp.mean(o * o, axis=-1, keepdims=True) + LN_EPS)
        or_ref[rows, :] = (o * g_ret * _silu(zr_ref[rows, :].astype(F32))).astype(or_ref.dtype)

        b = bg_ref[rows, :]
        b_mid = b[CHUNK // 2:CHUNK // 2 + 1, :]
        b_last = b[CHUNK - 1:CHUNK, :]
        bc = b - b_mid
        e_pos = jnp.exp(bc)
        e_neg = jnp.exp(-bc)
        q = qg_ref[rows, :].astype(F32)
        k = kg_ref[rows, :].astype(F32)
        v = vg_ref[rows, :]
        a_causal = _dot_nt((q * e_pos).astype(BF16), (k * e_neg).astype(BF16))
        a_anti = _dot_nt((q * e_neg).astype(BF16), (k * e_pos).astype(BF16))
        p = jnp.where(causal, a_causal, a_anti).astype(BF16)
        st_old = sg_ref[...]
        o = _dot(p, v) + _dot_nt((q * jnp.exp(b)).astype(BF16), st_old.astype(BF16))
        kb = (k * jnp.exp(b_last - b)).astype(BF16)
        sg_ref[...] = st_old * jnp.exp(b_last) + _dot_tn(v, kb)
        o = o * lax.rsqrt(jnp.mean(o * o, axis=-1, keepdims=True) + LN_EPS)
        og_ref[rows, :] = (o * g_gla * _silu(zg_ref[rows, :].astype(F32))).astype(og_ref.dtype)
        return carry

    lax.fori_loop(0, MIX_T // CHUNK, chunk, 0, unroll=MIX_UNROLL)


def _mixer(h, b_cum, dmat, qdec, kdec, sdec, g_ret, g_gla, batch, seq):
    m = h.shape[0]
    nt = seq // MIX_T
    rd = RET_DV
    c_kr = RET_QK // rd
    c_vr = 2 * RET_QK // rd
    c_zr = (2 * RET_QK + RET_V) // rd
    g0 = 2 * RET_QK + 2 * RET_V
    c_qg = g0 // GLA_DK
    c_kg = (g0 + GLA_QK) // GLA_DK
    c_vg = (g0 + 2 * GLA_QK) // rd
    c_zg = (g0 + 2 * GLA_QK + GLA_V) // rd

    def rows(bi, hi, ti):
        return bi * nt + ti

    def hspec(width, first):
        return pl.BlockSpec((MIX_T, width), lambda bi, hi, ti: (rows(bi, hi, ti), first + hi))

    def head_table(shape):
        return pl.BlockSpec((1,) + shape, lambda bi, hi, ti: (hi, 0, 0))

    out_spec = pl.BlockSpec((MIX_T, rd), lambda bi, hi, ti: (rows(bi, hi, ti), hi))
    return pl.pallas_call(
        _mixer_kernel,
        out_shape=(jax.ShapeDtypeStruct((m, RET_V), BF16), jax.ShapeDtypeStruct((m, GLA_V), BF16)),
        grid=(batch, RET_HEADS, nt),
        in_specs=[
            hspec(rd, 0), hspec(rd, c_kr), hspec(rd, c_vr), hspec(rd, c_zr),
            hspec(GLA_DK, c_qg), hspec(GLA_DK, c_kg), hspec(rd, c_vg), hspec(rd, c_zg),
            pl.BlockSpec((MIX_T, GLA_DK), lambda bi, hi, ti: (rows(bi, hi, ti), hi)),
            head_table((CHUNK, CHUNK)), head_table((CHUNK, RET_DV)), head_table((CHUNK, RET_DV)),
            head_table((1, RET_DV)),
            pl.BlockSpec((1, rd), lambda bi, hi, ti: (0, hi)),
            pl.BlockSpec((1, rd), lambda bi, hi, ti: (0, hi)),
        ],
        out_specs=(out_spec, out_spec),
        scratch_shapes=[pltpu.VMEM((RET_DK, RET_DV), F32), pltpu.VMEM((GLA_DV, GLA_DK), F32)],
        compiler_params=pltpu.CompilerParams(
            dimension_semantics=("parallel", "parallel", "arbitrary"), vmem_limit_bytes=VMEM_LIMIT),
        name="mixer",
    )(h, h, h, h, h, h, h, h, b_cum, dmat, qdec, kdec, sdec, g_ret, g_gla)


OUT_TM = 256


def _out_proj_kernel(or_ref, og_ref, w_ref, x_ref, g_ref, b_ref, y_ref, yb_ref):
    acc = _dot(or_ref[...], w_ref[:RET_V, :]) + _dot(og_ref[...], w_ref[RET_V:, :])
    y = _layer_norm(DEEPNORM_ALPHA * x_ref[...] + acc, g_ref[...], b_ref[...])
    y_ref[...] = y
    yb_ref[...] = y.astype(yb_ref.dtype)


def _out_proj(o_r, o_g, w_out, x2, g, b):
    m = x2.shape[0]
    row = lambda i: (i, 0)
    const = lambda i: (0, 0)
    return pl.pallas_call(
        _out_proj_kernel,
        out_shape=(jax.ShapeDtypeStruct((m, D_MODEL), F32), jax.ShapeDtypeStruct((m, D_MODEL), BF16)),
        grid=(m // OUT_TM,),
        in_specs=[
            pl.BlockSpec((OUT_TM, RET_V), row),
            pl.BlockSpec((OUT_TM, GLA_V), row),
            pl.BlockSpec((RET_V + GLA_V, D_MODEL), const),
            pl.BlockSpec((OUT_TM, D_MODEL), row),
            pl.BlockSpec((1, D_MODEL), const),
            pl.BlockSpec((1, D_MODEL), const),
        ],
        out_specs=(pl.BlockSpec((OUT_TM, D_MODEL), row), pl.BlockSpec((OUT_TM, D_MODEL), row)),
        compiler_params=pltpu.CompilerParams(
            dimension_semantics=("parallel",), vmem_limit_bytes=VMEM_LIMIT),
        name="out_proj_ln",
    )(o_r, o_g, w_out, x2, g, b)


UP_TM = 1024
UP_TF = 512
UP_SUB = 256
HEAD = 8


def _up_conv_kernel(x_ref, wv_ref, wg_ref, cv_ref, cg_ref, bv_ref, bg_ref,
                    o_ref, tail_v_ref, tail_g_ref, *, tiles_per_seq):
    i = pl.program_id(1)
    seq_start = i % tiles_per_seq == 0

    @pl.when(seq_start)
    def _():
        tail_v_ref[...] = jnp.zeros_like(tail_v_ref)
        tail_g_ref[...] = jnp.zeros_like(tail_g_ref)

    def conv(u, tail, c_ref, bias_ref):
        ext = jnp.concatenate([tail, u], axis=0)
        out = bias_ref[...]
        for tap in range(CONV_WIDTH):
            start = HEAD - (CONV_WIDTH - 1) + tap
            out = out + c_ref[tap:tap + 1, :] * ext[start:start + UP_SUB, :]
        return out, u[UP_SUB - HEAD:, :]

    tail_v = tail_v_ref[...]
    tail_g = tail_g_ref[...]
    for r in range(UP_TM // UP_SUB):
        rows = slice(r * UP_SUB, (r + 1) * UP_SUB)
        x = x_ref[rows, :]
        gate, tail_g = conv(_dot(x, wg_ref[...]), tail_g, cg_ref, bg_ref)
        val, tail_v = conv(_dot(x, wv_ref[...]), tail_v, cv_ref, bv_ref)
        o_ref[rows, :] = (_silu(gate) * val).astype(o_ref.dtype)
    tail_v_ref[...] = tail_v
    tail_g_ref[...] = tail_g


def _up_conv(x1b, w_up, w_conv, b_conv, seq):
    m = x1b.shape[0]
    nf = D_FF // UP_TF
    kern = functools.partial(_up_conv_kernel, tiles_per_seq=seq // UP_TM)
    return pl.pallas_call(
        kern,
        out_shape=jax.ShapeDtypeStruct((m, D_FF), BF16),
        grid=(nf, m // UP_TM),
        in_specs=[
            pl.BlockSpec((UP_TM, D_MODEL), lambda j, i: (i, 0)),
            pl.BlockSpec((D_MODEL, UP_TF), lambda j, i: (0, j)),
            pl.BlockSpec((D_MODEL, UP_TF), lambda j, i: (0, nf + j)),
            pl.BlockSpec((CONV_WIDTH, UP_TF), lambda j, i: (0, j)),
            pl.BlockSpec((CONV_WIDTH, UP_TF), lambda j, i: (0, nf + j)),
            pl.BlockSpec((1, UP_TF), lambda j, i: (0, j)),
            pl.BlockSpec((1, UP_TF), lambda j, i: (0, nf + j)),
        ],
        out_specs=pl.BlockSpec((UP_TM, UP_TF), lambda j, i: (i, j)),
        scratch_shapes=[pltpu.VMEM((HEAD, UP_TF), F32)] * 2,
        compiler_params=pltpu.CompilerParams(
            dimension_semantics=("arbitrary", "arbitrary"), vmem_limit_bytes=VMEM_LIMIT),
        name="up_conv_gate",
    )(x1b, w_up, w_up, w_conv, w_conv, b_conv, b_conv)


DOWN_TM = 256


def _down_proj_kernel(a_ref, w_ref, x_ref, g_ref, b_ref, y_ref):
    acc = _dot(a_ref[...], w_ref[...])
    y_ref[...] = _layer_norm(DEEPNORM_ALPHA * x_ref[...] + acc, g_ref[...], b_ref[...])


def _down_proj(act, w_down, x1, g, b):
    m = x1.shape[0]
    row = lambda i: (i, 0)
    const = lambda i: (0, 0)
    return pl.pallas_call(
        _down_proj_kernel,
        out_shape=jax.ShapeDtypeStruct((m, D_MODEL), F32),
        grid=(m // DOWN_TM,),
        in_specs=[
            pl.BlockSpec((DOWN_TM, D_FF), row),
            pl.BlockSpec((D_FF, D_MODEL), const, pipeline_mode=pl.Buffered(1)),
            pl.BlockSpec((DOWN_TM, D_MODEL), row),
            pl.BlockSpec((1, D_MODEL), const),
            pl.BlockSpec((1, D_MODEL), const),
        ],
        out_specs=pl.BlockSpec((DOWN_TM, D_MODEL), row),
        compiler_params=pltpu.CompilerParams(
            dimension_semantics=("parallel",), vmem_limit_bytes=VMEM_LIMIT),
        name="down_proj_ln",
    )(act, w_down, x1, g, b)


def _rotary_tables(seq):
    half = RET_DK // 2
    inv_freq = ROPE_BASE ** (-jnp.arange(half, dtype=F32) / half)
    ang = jnp.arange(seq, dtype=F32)[:, None] * inv_freq[None, :]
    return jnp.cos(ang), jnp.sin(ang)


def _retention_tables():
    log_gamma = jnp.log1p(-jnp.exp2(-5.0 - jnp.arange(RET_HEADS, dtype=F32)))
    idx = jnp.arange(CHUNK, dtype=F32)
    dmat = jnp.exp(log_gamma[:, None, None] * jnp.abs(idx[:, None] - idx[None, :]))
    qdec = jnp.exp(log_gamma[:, None] * (idx + 1.0))
    kdec = jnp.exp(log_gamma[:, None] * (CHUNK - 1.0 - idx))
    sdec = jnp.exp(log_gamma * CHUNK)
    qdec = jnp.broadcast_to(qdec[:, :, None], (RET_HEADS, CHUNK, RET_DV))
    kdec = jnp.broadcast_to(kdec[:, :, None], (RET_HEADS, CHUNK, RET_DV))
    sdec = jnp.broadcast_to(sdec[:, None, None], (RET_HEADS, 1, RET_DV))
    return dmat, qdec, kdec, sdec


def _chunk_tri(n):
    r = jnp.arange(n)
    same = (r[:, None] // CHUNK) == (r[None, :] // CHUNK)
    return (same & (r[:, None] >= r[None, :])).astype(BF16)


def kernel(x, w_in, w_gla_gate, b_gla_gate, g_ret, g_gla, w_out, ln1_g, ln1_b, w_up, w_conv,
           b_conv, w_down, ln2_g, ln2_b):
    batch, seq, d = x.shape
    m = batch * seq
    cos, sin = _rotary_tables(seq)
    dmat, qdec, kdec, sdec = _retention_tables()
    tri = _chunk_tri(TRI)
    x2 = x.reshape(m, d)
    for layer in range(DEPTH):
        w_in_l = w_in[layer]
        w_main = w_in_l[:, :H_WIDTH].astype(BF16)
        w_a = jnp.pad(w_in_l[:, H_WIDTH:], ((0, 0), (0, LANES - GLA_GATE_RANK))).astype(BF16)
        w_gate = jnp.pad(w_gla_gate[layer], ((0, LANES - GLA_GATE_RANK), (0, 0))).astype(BF16)
        h, b_cum = _in_proj(x2, w_main, cos, sin, w_a, w_gate,
                            b_gla_gate[layer].astype(F32)[None, :], tri, seq)
        o_r, o_g = _mixer(h, b_cum, dmat, qdec, kdec, sdec,
                          g_ret[layer].astype(F32)[None, :], g_gla[layer].astype(F32)[None, :],
                          batch, seq)
        x1, x1b = _out_proj(o_r, o_g, w_out[layer].astype(BF16), x2,
                            ln1_g[layer][None, :], ln1_b[layer][None, :])
        act = _up_conv(x1b, w_up[layer].astype(BF16), w_conv[layer], b_conv[layer][None, :], seq)
        x2 = _down_proj(act, w_down[layer].astype(BF16), x1,
                        ln2_g[layer][None, :], ln2_b[layer][None, :])
    return x2.reshape(batch, seq, d)
```

```python
import functools

import jax
import jax.numpy as jnp
from jax import lax
from jax.experimental import pallas as pl
from jax.experimental.pallas import tpu as pltpu

D_MODEL = 2048
CHUNK = 64
RET_HEADS = 4
RET_DK = 256
RET_DV = 256
GLA_HEADS = 4
GLA_DK = 128
GLA_DV = 256
GLA_GATE_RANK = 16
GLA_GATE_TAU = 16.0
ROPE_BASE = 10000.0
D_FF = 5632
CONV_WIDTH = 3
LN_EPS = 1e-5
DEPTH = 1
DEEPNORM_ALPHA = (2.0 * DEPTH) ** 0.25

RET_QK = RET_HEADS * RET_DK
RET_V = RET_HEADS * RET_DV
GLA_QK = GLA_HEADS * GLA_DK
GLA_V = GLA_HEADS * GLA_DV
H_WIDTH = 2 * RET_QK + 2 * RET_V + 2 * GLA_QK + 2 * GLA_V

LANES = 128
VMEM_LIMIT = 56 * 1024 * 1024

F32 = jnp.float32
BF16 = jnp.bfloat16


def _dot(a, b):
    return jnp.dot(a, b, preferred_element_type=F32)


def _dot_nt(a, b):
    return lax.dot_general(a, b, (((1,), (1,)), ((), ())), preferred_element_type=F32)


def _dot_tn(a, b):
    return lax.dot_general(a, b, (((0,), (0,)), ((), ())), preferred_element_type=F32)


def _silu(z):
    return z / (1.0 + jnp.exp(-z))


def _layer_norm(y, g, b):
    mu = jnp.mean(y, axis=-1, keepdims=True)
    yc = y - mu
    var = jnp.mean(yc * yc, axis=-1, keepdims=True)
    return yc * lax.rsqrt(var + LN_EPS) * g + b


IN_TM = 1024
IN_TN = 1024
IN_SUB = 256
TRI = 256


def _in_proj_kernel(x_ref, w_ref, cos_ref, sin_ref, wa_ref, wg_ref, bias_ref, tri_ref,
                    o_ref, b_ref, xb_ref):
    j = pl.program_id(1)

    @pl.when(j == 0)
    def _():
        xb_ref[...] = x_ref[...].astype(xb_ref.dtype)
        a = _dot(xb_ref[...], wa_ref[...])
        z = _dot(a.astype(BF16), wg_ref[...]) + bias_ref[...]
        log_a = (jnp.minimum(z, 0.0) - jnp.log1p(jnp.exp(-jnp.abs(z)))) * (1.0 / GLA_GATE_TAU)
        hi = log_a.astype(BF16)
        lo = (log_a - hi.astype(F32)).astype(BF16)
        tri = tri_ref[...]
        for r in range(IN_TM // TRI):
            rows = slice(r * TRI, (r + 1) * TRI)
            b_ref[rows, :] = _dot(tri, hi[rows, :]) + _dot(tri, lo[rows, :])

    def rotary(acc, rows, scale):
        cos = cos_ref[rows, :]
        sin = sin_ref[rows, :]
        for hd in range(IN_TN // RET_DK):
            lo = hd * RET_DK
            x1 = acc[:, lo:lo + LANES]
            x2 = acc[:, lo + LANES:lo + 2 * LANES]
            o_ref[rows, lo:lo + LANES] = ((x1 * cos - x2 * sin) * scale).astype(o_ref.dtype)
            o_ref[rows, lo + LANES:lo + 2 * LANES] = ((x1 * sin + x2 * cos) * scale).astype(o_ref.dtype)

    def gla_qk(acc, rows):
        o_ref[rows, :GLA_QK] = (acc[:, :GLA_QK] * (GLA_DK ** -0.5)).astype(o_ref.dtype)
        o_ref[rows, GLA_QK:] = acc[:, GLA_QK:].astype(o_ref.dtype)

    def plain(acc, rows):
        o_ref[rows, :] = acc.astype(o_ref.dtype)

    def project(epilogue):
        for r in range(IN_TM // IN_SUB):
            rows = slice(r * IN_SUB, (r + 1) * IN_SUB)
            epilogue(_dot(xb_ref[rows, :], w_ref[...]), rows)

    @pl.when(j == 0)
    def _():
        project(functools.partial(rotary, scale=1.0))

    @pl.when(j == 1)
    def _():
        project(functools.partial(rotary, scale=RET_DK ** -0.5))

    @pl.when(j == 4)
    def _():
        project(gla_qk)

    @pl.when((j == 2) | (j == 3) | (j >= 5))
    def _():
        project(plain)


def _in_proj(x2, w_main, cos, sin, w_a, w_gate, bias, tri, seq):
    m = x2.shape[0]
    nseq = seq // IN_TM
    const = lambda i, j: (0, 0)
    return pl.pallas_call(
        _in_proj_kernel,
        out_shape=(jax.ShapeDtypeStruct((m, H_WIDTH), BF16), jax.ShapeDtypeStruct((m, GLA_QK), F32)),
        grid=(m // IN_TM, H_WIDTH // IN_TN),
        in_specs=[
            pl.BlockSpec((IN_TM, D_MODEL), lambda i, j: (i, 0)),
            pl.BlockSpec((D_MODEL, IN_TN), lambda i, j: (0, j)),
            pl.BlockSpec((IN_TM, LANES), lambda i, j: (i % nseq, 0)),
            pl.BlockSpec((IN_TM, LANES), lambda i, j: (i % nseq, 0)),
            pl.BlockSpec((D_MODEL, LANES), const),
            pl.BlockSpec((LANES, GLA_QK), const),
            pl.BlockSpec((1, GLA_QK), const),
            pl.BlockSpec((TRI, TRI), const),
        ],
        out_specs=(pl.BlockSpec((IN_TM, IN_TN), lambda i, j: (i, j)),
                   pl.BlockSpec((IN_TM, GLA_QK), lambda i, j: (i, 0))),
        scratch_shapes=[pltpu.VMEM((IN_TM, D_MODEL), BF16)],
        compiler_params=pltpu.CompilerParams(
            dimension_semantics=("parallel", "arbitrary"), vmem_limit_bytes=VMEM_LIMIT),
        name="in_proj",
    )(x2, w_main, cos, sin, w_a, w_gate, bias, tri)


MIX_T = 512
RET_BLOCK = 256


def _mixer_kernel(qr_ref, kr_ref, vr_ref, zr_ref, qg_ref, kg_ref, vg_ref, zg_ref, bg_ref,
                  dmat_ref, qdec_ref, kdec_ref, sdec_ref, gr_ref, gg_ref,
                  or_ref, og_ref, sr_ref, sg_ref):
    @pl.when(pl.program_id(2) == 0)
    def _():
        sr_ref[...] = jnp.zeros_like(sr_ref)
        sg_ref[...] = jnp.zeros_like(sg_ref)

    g_ret = gr_ref[...]
    g_gla = gg_ref[...]
    row = lax.broadcasted_iota(jnp.int32, (CHUNK, CHUNK), 0)
    col = lax.broadcasted_iota(jnp.int32, (CHUNK, CHUNK), 1)
    causal = row >= col

    def ret_block(r0):
        rows = slice(r0, r0 + RET_BLOCK)
        q = qr_ref[rows, :]
        k = kr_ref[rows, :]
        v = vr_ref[rows, :]
        p = (_dot_nt(q, k) * dmat_ref[0]).astype(BF16)
        s_old = sr_ref[...]
        o = _dot(p, v) + qdec_ref[0] * _dot(q, s_old.astype(BF16))
        vk = (v.astype(F32) * kdec_ref[0]).astype(BF16)
        sr_ref[...] = s_old * sdec_ref[0] + _dot_tn(k, vk)
        o = o - jnp.mean(o, axis=-1, keepdims=True)
        o = o * lax.rsqrt(jnp.mean(o * o, axis=-1, keepdims=True) + LN_EPS)
        or_ref[rows, :] = (o * g_ret * _silu(zr_ref[rows, :].astype(F32))).astype(or_ref.dtype)

    def gla_chunk(r0):
        rows = slice(r0, r0 + CHUNK)
        b = bg_ref[rows, :]
        b_mid = b[CHUNK // 2:CHUNK // 2 + 1, :]
        b_last = b[CHUNK - 1:CHUNK, :]
        bc = b - b_mid
        e_pos = jnp.exp(bc)
        e_neg = jnp.exp(-bc)
        q = qg_ref[rows, :].astype(F32)
        k = kg_ref[rows, :].astype(F32)
        v = vg_ref[rows, :]
        a_causal = _dot_nt((q * e_pos).astype(BF16), (k * e_neg).astype(BF16))
        a_anti = _dot_nt((q * e_neg).astype(BF16), (k * e_pos).astype(BF16))
        p = jnp.where(causal, a_causal, a_anti).astype(BF16)
        st_old = sg_ref[...]
        o = _dot(p, v) + _dot_nt((q * jnp.exp(b)).astype(BF16), st_old.astype(BF16))
        kb = (k * jnp.exp(b_last - b)).astype(BF16)
        sg_ref[...] = st_old * jnp.exp(b_last) + _dot_tn(v, kb)
        o = o * lax.rsqrt(jnp.mean(o * o, axis=-1, keepdims=True) + LN_EPS)
        og_ref[rows, :] = (o * g_gla * _silu(zg_ref[rows, :].astype(F32))).astype(og_ref.dtype)

    for r0 in range(0, MIX_T, RET_BLOCK):
        ret_block(r0)
        for c0 in range(r0, r0 + RET_BLOCK, CHUNK):
            gla_chunk(c0)


def _mixer(h, b_cum, dmat, qdec, kdec, sdec, g_ret, g_gla, batch, seq):
    m = h.shape[0]
    nt = seq // MIX_T
    rd = RET_DV
    c_kr = RET_QK // rd
    c_vr = 2 * RET_QK // rd
    c_zr = (2 * RET_QK + RET_V) // rd
    g0 = 2 * RET_QK + 2 * RET_V
    c_qg = g0 // GLA_DK
    c_kg = (g0 + GLA_QK) // GLA_DK
    c_vg = (g0 + 2 * GLA_QK) // rd
    c_zg = (g0 + 2 * GLA_QK + GLA_V) // rd

    def rows(bi, hi, ti):
        return bi * nt + ti

    def hspec(width, first):
        return pl.BlockSpec((MIX_T, width), lambda bi, hi, ti: (rows(bi, hi, ti), first + hi))

    def head_table(shape):
        return pl.BlockSpec((1,) + shape, lambda bi, hi, ti: (hi, 0, 0))

    out_spec = pl.BlockSpec((MIX_T, rd), lambda bi, hi, ti: (rows(bi, hi, ti), hi))
    return pl.pallas_call(
        _mixer_kernel,
        out_shape=(jax.ShapeDtypeStruct((m, RET_V), BF16), jax.ShapeDtypeStruct((m, GLA_V), BF16)),
        grid=(batch, RET_HEADS, nt),
        in_specs=[
            hspec(rd, 0), hspec(rd, c_kr), hspec(rd, c_vr), hspec(rd, c_zr),
            hspec(GLA_DK, c_qg), hspec(GLA_DK, c_kg), hspec(rd, c_vg), hspec(rd, c_zg),
            pl.BlockSpec((MIX_T, GLA_DK), lambda bi, hi, ti: (rows(bi, hi, ti), hi)),
            head_table((RET_BLOCK, RET_BLOCK)), head_table((RET_BLOCK, RET_DV)),
            head_table((RET_BLOCK, RET_DV)), head_table((1, RET_DV)),
            pl.BlockSpec((1, rd), lambda bi, hi, ti: (0, hi)),
            pl.BlockSpec((1, rd), lambda bi, hi, ti: (0, hi)),
        ],
        out_specs=(out_spec, out_spec),
        scratch_shapes=[pltpu.VMEM((RET_DK, RET_DV), F32), pltpu.VMEM((GLA_DV, GLA_DK), F32)],
        compiler_params=pltpu.CompilerParams(
            dimension_semantics=("parallel", "parallel", "arbitrary"), vmem_limit_bytes=VMEM_LIMIT),
        name="mixer",
    )(h, h, h, h, h, h, h, h, b_cum, dmat, qdec, kdec, sdec, g_ret, g_gla)


OUT_TM = 512
OUT_SUB = 128


def _out_proj_kernel(or_ref, og_ref, w_ref, x_ref, g_ref, b_ref, y_ref, yb_ref):
    for r in range(OUT_TM // OUT_SUB):
        rows = slice(r * OUT_SUB, (r + 1) * OUT_SUB)
        acc = _dot(or_ref[rows, :], w_ref[:RET_V, :]) + _dot(og_ref[rows, :], w_ref[RET_V:, :])
        y = _layer_norm(DEEPNORM_ALPHA * x_ref[rows, :] + acc, g_ref[...], b_ref[...])
        y_ref[rows, :] = y
        yb_ref[rows, :] = y.astype(yb_ref.dtype)


def _out_proj(o_r, o_g, w_out, x2, g, b):
    m = x2.shape[0]
    row = lambda i: (i, 0)
    const = lambda i: (0, 0)
    return pl.pallas_call(
        _out_proj_kernel,
        out_shape=(jax.ShapeDtypeStruct((m, D_MODEL), F32), jax.ShapeDtypeStruct((m, D_MODEL), BF16)),
        grid=(m // OUT_TM,),
        in_specs=[
            pl.BlockSpec((OUT_TM, RET_V), row),
            pl.BlockSpec((OUT_TM, GLA_V), row),
            pl.BlockSpec((RET_V + GLA_V, D_MODEL), const),
            pl.BlockSpec((OUT_TM, D_MODEL), row),
            pl.BlockSpec((1, D_MODEL), const),
            pl.BlockSpec((1, D_MODEL), const),
        ],
        out_specs=(pl.BlockSpec((OUT_TM, D_MODEL), row), pl.BlockSpec((OUT_TM, D_MODEL), row)),
        compiler_params=pltpu.CompilerParams(
            dimension_semantics=("parallel",), vmem_limit_bytes=VMEM_LIMIT),
        name="out_proj_ln",
    )(o_r, o_g, w_out, x2, g, b)


UP_TM = 1024
UP_TF = 512
UP_SUB = 128
HALO = 16


def _up_conv_kernel(x_ref, halo_ref, wv_ref, wg_ref, cv_ref, cg_ref, bv_ref, bg_ref,
                    o_ref, lhs_ref, uv_ref, ug_ref, *, tiles_per_seq):
    i = pl.program_id(0)
    j = pl.program_id(1)

    @pl.when(j == 0)
    def _():
        keep = (i % tiles_per_seq != 0).astype(halo_ref.dtype)
        lhs_ref[:HALO, :] = halo_ref[...] * keep
        lhs_ref[HALO:, :] = x_ref[...]

    def conv(u_ref, r0, c_ref, bias_ref):
        out = bias_ref[...]
        for tap in range(CONV_WIDTH):
            start = r0 + HALO - (CONV_WIDTH - 1) + tap
            out = out + c_ref[tap:tap + 1, :] * u_ref[start:start + UP_SUB, :]
        return out

    for r in range(UP_TM // UP_SUB):
        r0 = r * UP_SUB
        lo = 0 if r == 0 else r0 + HALO
        rows = slice(lo, r0 + HALO + UP_SUB)
        lhs = lhs_ref[rows, :]
        ug_ref[rows, :] = _dot(lhs, wg_ref[...])
        uv_ref[rows, :] = _dot(lhs, wv_ref[...])
        gate = conv(ug_ref, r0, cg_ref, bg_ref)
        val = conv(uv_ref, r0, cv_ref, bv_ref)
        o_ref[r0:r0 + UP_SUB, :] = (_silu(gate) * val).astype(o_ref.dtype)


def _up_conv(x1b, w_up, w_conv, b_conv, seq):
    m = x1b.shape[0]
    nf = D_FF // UP_TF
    tiles_per_seq = seq // UP_TM
    halo_blocks = UP_TM // HALO
    kern = functools.partial(_up_conv_kernel, tiles_per_seq=tiles_per_seq)
    return pl.pallas_call(
        kern,
        out_shape=jax.ShapeDtypeStruct((m, D_FF), BF16),
        grid=(m // UP_TM, nf),
        in_specs=[
            pl.BlockSpec((UP_TM, D_MODEL), lambda i, j: (i, 0)),
            pl.BlockSpec((HALO, D_MODEL), lambda i, j: (jnp.maximum(i * halo_blocks - 1, 0), 0)),
            pl.BlockSpec((D_MODEL, UP_TF), lambda i, j: (0, j)),
            pl.BlockSpec((D_MODEL, UP_TF), lambda i, j: (0, nf + j)),
            pl.BlockSpec((CONV_WIDTH, UP_TF), lambda i, j: (0, j)),
            pl.BlockSpec((CONV_WIDTH, UP_TF), lambda i, j: (0, nf + j)),
            pl.BlockSpec((1, UP_TF), lambda i, j: (0, j)),
            pl.BlockSpec((1, UP_TF), lambda i, j: (0, nf + j)),
        ],
        out_specs=pl.BlockSpec((UP_TM, UP_TF), lambda i, j: (i, j)),
        scratch_shapes=[
            pltpu.VMEM((UP_TM + HALO, D_MODEL), BF16),
            pltpu.VMEM((UP_TM + HALO, UP_TF), F32),
            pltpu.VMEM((UP_TM + HALO, UP_TF), F32),
        ],
        compiler_params=pltpu.CompilerParams(
            dimension_semantics=("parallel", "arbitrary"), vmem_limit_bytes=VMEM_LIMIT),
        name="up_conv_gate",
    )(x1b, x1b, w_up, w_up, w_conv, w_conv, b_conv, b_conv)


DOWN_TM = 512
DOWN_SUB = 128


def _down_proj_kernel(a_ref, w_ref, x_ref, g_ref, b_ref, y_ref):
    for r in range(DOWN_TM // DOWN_SUB):
        rows = slice(r * DOWN_SUB, (r + 1) * DOWN_SUB)
        acc = _dot(a_ref[rows, :], w_ref[...])
        y_ref[rows, :] = _layer_norm(DEEPNORM_ALPHA * x_ref[rows, :] + acc, g_ref[...], b_ref[...])


def _down_proj(act, w_down, x1, g, b):
    m = x1.shape[0]
    row = lambda i: (i, 0)
    const = lambda i: (0, 0)
    return pl.pallas_call(
        _down_proj_kernel,
        out_shape=jax.ShapeDtypeStruct((m, D_MODEL), F32),
        grid=(m // DOWN_TM,),
        in_specs=[
            pl.BlockSpec((DOWN_TM, D_FF), row),
            pl.BlockSpec((D_FF, D_MODEL), const, pipeline_mode=pl.Buffered(1)),
            pl.BlockSpec((DOWN_TM, D_MODEL), row),
            pl.BlockSpec((1, D_MODEL), const),
            pl.BlockSpec((1, D_MODEL), const),
        ],
        out_specs=pl.BlockSpec((DOWN_TM, D_MODEL), row),
        compiler_params=pltpu.CompilerParams(
            dimension_semantics=("parallel",), vmem_limit_bytes=VMEM_LIMIT),
        name="down_proj_ln",
    )(act, w_down, x1, g, b)


def _rotary_tables(seq):
    half = RET_DK // 2
    inv_freq = ROPE_BASE ** (-jnp.arange(half, dtype=F32) / half)
    ang = jnp.arange(seq, dtype=F32)[:, None] * inv_freq[None, :]
    return jnp.cos(ang), jnp.sin(ang)


def _retention_tables():
    log_gamma = jnp.log1p(-jnp.exp2(-5.0 - jnp.arange(RET_HEADS, dtype=F32)))
    idx = jnp.arange(RET_BLOCK, dtype=F32)
    dist = idx[:, None] - idx[None, :]
    chunk_gap = idx[:, None] // CHUNK - idx[None, :] // CHUNK
    expo = jnp.where(chunk_gap == 0, jnp.abs(dist), dist)
    dmat = jnp.where(chunk_gap >= 0, jnp.exp(log_gamma[:, None, None] * expo), 0.0)
    qdec = jnp.exp(log_gamma[:, None] * (idx + 1.0))
    kdec = jnp.exp(log_gamma[:, None] * (RET_BLOCK - 1.0 - idx))
    sdec = jnp.exp(log_gamma * RET_BLOCK)
    qdec = jnp.broadcast_to(qdec[:, :, None], (RET_HEADS, RET_BLOCK, RET_DV))
    kdec = jnp.broadcast_to(kdec[:, :, None], (RET_HEADS, RET_BLOCK, RET_DV))
    sdec = jnp.broadcast_to(sdec[:, None, None], (RET_HEADS, 1, RET_DV))
    return dmat, qdec, kdec, sdec


def _chunk_tri(n):
    r = jnp.arange(n)
    same = (r[:, None] // CHUNK) == (r[None, :] // CHUNK)
    return (same & (r[:, None] >= r[None, :])).astype(BF16)


def kernel(x, w_in, w_gla_gate, b_gla_gate, g_ret, g_gla, w_out, ln1_g, ln1_b, w_up, w_conv,
           b_conv, w_down, ln2_g, ln2_b):
    batch, seq, d = x.shape
    m = batch * seq
    cos, sin = _rotary_tables(seq)
    dmat, qdec, kdec, sdec = _retention_tables()
    tri = _chunk_tri(TRI)
    x2 = x.reshape(m, d)
    for layer in range(DEPTH):
        w_in_l = w_in[layer]
        w_main = w_in_l[:, :H_WIDTH].astype(BF16)
        w_a = jnp.pad(w_in_l[:, H_WIDTH:], ((0, 0), (0, LANES - GLA_GATE_RANK))).astype(BF16)
        w_gate = jnp.pad(w_gla_gate[layer], ((0, LANES - GLA_GATE_RANK), (0, 0))).astype(BF16)
        h, b_cum = _in_proj(x2, w_main, cos, sin, w_a, w_gate,
                            b_gla_gate[layer].astype(F32)[None, :], tri, seq)
        o_r, o_g = _mixer(h, b_cum, dmat, qdec, kdec, sdec,
                          g_ret[layer].astype(F32)[None, :], g_gla[layer].astype(F32)[None, :],
                          batch, seq)
        x1, x1b = _out_proj(o_r, o_g, w_out[layer].astype(BF16), x2,
                            ln1_g[layer][None, :], ln1_b[layer][None, :])
        act = _up_conv(x1b, w_up[layer].astype(BF16), w_conv[layer], b_conv[layer][None, :], seq)
        x2 = _down_proj(act, w_down[layer].astype(BF16), x1,
                        ln2_g[layer][None, :], ln2_b[layer][None, :])
    return x2.reshape(batch, seq, d)
```

```python
import functools

import jax
import jax.numpy as jnp
from jax import lax
from jax.experimental import pallas as pl
from jax.experimental.pallas import tpu as pltpu

D_MODEL = 2048
CHUNK = 64
RET_HEADS = 4
RET_DK = 256
RET_DV = 256
GLA_HEADS = 4
GLA_DK = 128
GLA_DV = 256
GLA_GATE_RANK = 16
GLA_GATE_TAU = 16.0
ROPE_BASE = 10000.0
D_FF = 5632
CONV_WIDTH = 3
LN_EPS = 1e-5
DEPTH = 1
DEEPNORM_ALPHA = (2.0 * DEPTH) ** 0.25

RET_QK = RET_HEADS * RET_DK
RET_V = RET_HEADS * RET_DV
GLA_QK = GLA_HEADS * GLA_DK
GLA_V = GLA_HEADS * GLA_DV
H_WIDTH = 2 * RET_QK + 2 * RET_V + 2 * GLA_QK + 2 * GLA_V

LANES = 128
VMEM_LIMIT = 56 * 1024 * 1024

F32 = jnp.float32
BF16 = jnp.bfloat16


def _dot(a, b):
    return jnp.dot(a, b, preferred_element_type=F32)


def _dot_nt(a, b):
    return lax.dot_general(a, b, (((1,), (1,)), ((), ())), preferred_element_type=F32)


def _dot_tn(a, b):
    return lax.dot_general(a, b, (((0,), (0,)), ((), ())), preferred_element_type=F32)


def _silu(z):
    return z / (1.0 + jnp.exp(-z))


def _layer_norm(y, g, b):
    mu = jnp.mean(y, axis=-1, keepdims=True)
    yc = y - mu
    var = jnp.mean(yc * yc, axis=-1, keepdims=True)
    return yc * lax.rsqrt(var + LN_EPS) * g + b


IN_TM = 1024
IN_TN = 1024
IN_SUB = 256
TRI = 256


def _in_proj_kernel(x_ref, w_ref, cos_ref, sin_ref, wa_ref, wg_ref, bias_ref, tri_ref,
                    o_ref, b_ref, xb_ref):
    j = pl.program_id(1)

    @pl.when(j == 0)
    def _():
        xb_ref[...] = x_ref[...].astype(xb_ref.dtype)
        a = _dot(xb_ref[...], wa_ref[...])
        z = _dot(a.astype(BF16), wg_ref[...]) + bias_ref[...]
        log_a = (jnp.minimum(z, 0.0) - jnp.log1p(jnp.exp(-jnp.abs(z)))) * (1.0 / GLA_GATE_TAU)
        hi = log_a.astype(BF16)
        lo = (log_a - hi.astype(F32)).astype(BF16)
        tri = tri_ref[...]
        for r in range(IN_TM // TRI):
            rows = slice(r * TRI, (r + 1) * TRI)
            b_ref[rows, :] = _dot(tri, hi[rows, :]) + _dot(tri, lo[rows, :])

    def rotary(acc, rows, scale):
        cos = cos_ref[rows, :]
        sin = sin_ref[rows, :]
        for hd in range(IN_TN // RET_DK):
            lo = hd * RET_DK
            x1 = acc[:, lo:lo + LANES]
            x2 = acc[:, lo + LANES:lo + 2 * LANES]
            o_ref[rows, lo:lo + LANES] = ((x1 * cos - x2 * sin) * scale).astype(o_ref.dtype)
            o_ref[rows, lo + LANES:lo + 2 * LANES] = ((x1 * sin + x2 * cos) * scale).astype(o_ref.dtype)

    def gla_qk(acc, rows):
        o_ref[rows, :GLA_QK] = (acc[:, :GLA_QK] * (GLA_DK ** -0.5)).astype(o_ref.dtype)
        o_ref[rows, GLA_QK:] = acc[:, GLA_QK:].astype(o_ref.dtype)

    def plain(acc, rows):
        o_ref[rows, :] = acc.astype(o_ref.dtype)

    def project(epilogue):
        for r in range(IN_TM // IN_SUB):
            rows = slice(r * IN_SUB, (r + 1) * IN_SUB)
            epilogue(_dot(xb_ref[rows, :], w_ref[...]), rows)

    @pl.when(j == 0)
    def _():
        project(functools.partial(rotary, scale=1.0))

    @pl.when(j == 1)
    def _():
        project(functools.partial(rotary, scale=RET_DK ** -0.5))

    @pl.when(j == 4)
    def _():
        project(gla_qk)

    @pl.when((j == 2) | (j == 3) | (j >= 5))
    def _():
        project(plain)


def _in_proj(x2, w_main, cos, sin, w_a, w_gate, bias, tri, seq):
    m = x2.shape[0]
    nseq = seq // IN_TM
    const = lambda i, j: (0, 0)
    return pl.pallas_call(
        _in_proj_kernel,
        out_shape=(jax.ShapeDtypeStruct((m, H_WIDTH), BF16), jax.ShapeDtypeStruct((m, GLA_QK), F32)),
        grid=(m // IN_TM, H_WIDTH // IN_TN),
        in_specs=[
            pl.BlockSpec((IN_TM, D_MODEL), lambda i, j: (i, 0)),
            pl.BlockSpec((D_MODEL, IN_TN), lambda i, j: (0, j)),
            pl.BlockSpec((IN_TM, LANES), lambda i, j: (i % nseq, 0)),
            pl.BlockSpec((IN_TM, LANES), lambda i, j: (i % nseq, 0)),
            pl.BlockSpec((D_MODEL, LANES), const),
            pl.BlockSpec((LANES, GLA_QK), const),
            pl.BlockSpec((1, GLA_QK), const),
            pl.BlockSpec((TRI, TRI), const),
        ],
        out_specs=(pl.BlockSpec((IN_TM, IN_TN), lambda i, j: (i, j)),
                   pl.BlockSpec((IN_TM, GLA_QK), lambda i, j: (i, 0))),
        scratch_shapes=[pltpu.VMEM((IN_TM, D_MODEL), BF16)],
        compiler_params=pltpu.CompilerParams(
            dimension_semantics=("parallel", "arbitrary"), vmem_limit_bytes=VMEM_LIMIT),
        name="in_proj",
    )(x2, w_main, cos, sin, w_a, w_gate, bias, tri)


MIX_T = 1024
RET_BLOCK = 256


def _mixer_kernel(qr_ref, kr_ref, vr_ref, zr_ref, qg_ref, kg_ref, vg_ref, zg_ref, bg_ref,
                  dmat_ref, qdec_ref, kdec_ref, sdec_ref, gr_ref, gg_ref,
                  or_ref, og_ref, sr_ref, sg_ref):
    @pl.when(pl.program_id(2) == 0)
    def _():
        sr_ref[...] = jnp.zeros_like(sr_ref)
        sg_ref[...] = jnp.zeros_like(sg_ref)

    g_ret = gr_ref[...]
    g_gla = gg_ref[...]
    row = lax.broadcasted_iota(jnp.int32, (CHUNK, CHUNK), 0)
    col = lax.broadcasted_iota(jnp.int32, (CHUNK, CHUNK), 1)
    causal = row >= col

    def ret_block(r0):
        rows = slice(r0, r0 + RET_BLOCK)
        q = qr_ref[rows, :]
        k = kr_ref[rows, :]
        v = vr_ref[rows, :]
        p = (_dot_nt(q, k) * dmat_ref[0]).astype(BF16)
        s_old = sr_ref[...]
        o = _dot(p, v) + qdec_ref[0] * _dot(q, s_old.astype(BF16))
        vk = (v.astype(F32) * kdec_ref[0]).astype(BF16)
        sr_ref[...] = s_old * sdec_ref[0] + _dot_tn(k, vk)
        o = o - jnp.mean(o, axis=-1, keepdims=True)
        o = o * lax.rsqrt(jnp.mean(o * o, axis=-1, keepdims=True) + LN_EPS)
        or_ref[rows, :] = (o * g_ret * _silu(zr_ref[rows, :].astype(F32))).astype(or_ref.dtype)

    def gla_chunk(r0):
        rows = slice(r0, r0 + CHUNK)
        b = bg_ref[rows, :]
        b_mid = b[CHUNK // 2:CHUNK // 2 + 1, :]
        b_last = b[CHUNK - 1:CHUNK, :]
        bc = b - b_mid
        e_pos = jnp.exp(bc)
        e_neg = jnp.exp(-bc)
        q = qg_ref[rows, :].astype(F32)
        k = kg_ref[rows, :].astype(F32)
        v = vg_ref[rows, :]
        a_causal = _dot_nt((q * e_pos).astype(BF16), (k * e_neg).astype(BF16))
        a_anti = _dot_nt((q * e_neg).astype(BF16), (k * e_pos).astype(BF16))
        p = jnp.where(causal, a_causal, a_anti).astype(BF16)
        st_old = sg_ref[...]
        o = _dot(p, v) + _dot_nt((q * jnp.exp(b)).astype(BF16), st_old.astype(BF16))
        kb = (k * jnp.exp(b_last - b)).astype(BF16)
        sg_ref[...] = st_old * jnp.exp(b_last) + _dot_tn(v, kb)
        o = o * lax.rsqrt(jnp.mean(o * o, axis=-1, keepdims=True) + LN_EPS)
        og_ref[rows, :] = (o * g_gla * _silu(zg_ref[rows, :].astype(F32))).astype(og_ref.dtype)

    for r0 in range(0, MIX_T, RET_BLOCK):
        ret_block(r0)
        for c0 in range(r0, r0 + RET_BLOCK, CHUNK):
            gla_chunk(c0)


def _mixer(h, b_cum, dmat, qdec, kdec, sdec, g_ret, g_gla, batch, seq):
    m = h.shape[0]
    nt = seq // MIX_T
    rd = RET_DV
    c_kr = RET_QK // rd
    c_vr = 2 * RET_QK // rd
    c_zr = (2 * RET_QK + RET_V) // rd
    g0 = 2 * RET_QK + 2 * RET_V
    c_qg = g0 // GLA_DK
    c_kg = (g0 + GLA_QK) // GLA_DK
    c_vg = (g0 + 2 * GLA_QK) // rd
    c_zg = (g0 + 2 * GLA_QK + GLA_V) // rd

    def rows(bi, hi, ti):
        return bi * nt + ti

    def hspec(width, first):
        return pl.BlockSpec((MIX_T, width), lambda bi, hi, ti: (rows(bi, hi, ti), first + hi))

    def head_table(shape):
        return pl.BlockSpec((1,) + shape, lambda bi, hi, ti: (hi, 0, 0))

    out_spec = pl.BlockSpec((MIX_T, rd), lambda bi, hi, ti: (rows(bi, hi, ti), hi))
    return pl.pallas_call(
        _mixer_kernel,
        out_shape=(jax.ShapeDtypeStruct((m, RET_V), BF16), jax.ShapeDtypeStruct((m, GLA_V), BF16)),
        grid=(batch, RET_HEADS, nt),
        in_specs=[
            hspec(rd, 0), hspec(rd, c_kr), hspec(rd, c_vr), hspec(rd, c_zr),
            hspec(GLA_DK, c_qg), hspec(GLA_DK, c_kg), hspec(rd, c_vg), hspec(rd, c_zg),
            pl.BlockSpec((MIX_T, GLA_DK), lambda bi, hi, ti: (rows(bi, hi, ti), hi)),
            head_table((RET_BLOCK, RET_BLOCK)), head_table((RET_BLOCK, RET_DV)),
            head_table((RET_BLOCK, RET_DV)), head_table((1, RET_DV)),
            pl.BlockSpec((1, rd), lambda bi, hi, ti: (0, hi)),
            pl.BlockSpec((1, rd), lambda bi, hi, ti: (0, hi)),
        ],
        out_specs=(out_spec, out_spec),
        scratch_shapes=[pltpu.VMEM((RET_DK, RET_DV), F32), pltpu.VMEM((GLA_DV, GLA_DK), F32)],
        compiler_params=pltpu.CompilerParams(
            dimension_semantics=("parallel", "parallel", "arbitrary"), vmem_limit_bytes=VMEM_LIMIT),
        name="mixer",
    )(h, h, h, h, h, h, h, h, b_cum, dmat, qdec, kdec, sdec, g_ret, g_gla)


OUT_TM = 512
OUT_SUB = 128


def _out_proj_kernel(or_ref, og_ref, w_ref, x_ref, g_ref, b_ref, y_ref, yb_ref):
    for r in range(OUT_TM // OUT_SUB):
        rows = slice(r * OUT_SUB, (r + 1) * OUT_SUB)
        acc = _dot(or_ref[rows, :], w_ref[:RET_V, :]) + _dot(og_ref[rows, :], w_ref[RET_V:, :])
        y = _layer_norm(DEEPNORM_ALPHA * x_ref[rows, :] + acc, g_ref[...], b_ref[...])
        y_ref[rows, :] = y
        yb_ref[rows, :] = y.astype(yb_ref.dtype)


def _out_proj(o_r, o_g, w_out, x2, g, b):
    m = x2.shape[0]
    row = lambda i: (i, 0)
    const = lambda i: (0, 0)
    return pl.pallas_call(
        _out_proj_kernel,
        out_shape=(jax.ShapeDtypeStruct((m, D_MODEL), F32), jax.ShapeDtypeStruct((m, D_MODEL), BF16)),
        grid=(m // OUT_TM,),
        in_specs=[
            pl.BlockSpec((OUT_TM, RET_V), row),
            pl.BlockSpec((OUT_TM, GLA_V), row),
            pl.BlockSpec((RET_V + GLA_V, D_MODEL), const),
            pl.BlockSpec((OUT_TM, D_MODEL), row),
            pl.BlockSpec((1, D_MODEL), const),
            pl.BlockSpec((1, D_MODEL), const),
        ],
        out_specs=(pl.BlockSpec((OUT_TM, D_MODEL), row), pl.BlockSpec((OUT_TM, D_MODEL), row)),
        compiler_params=pltpu.CompilerParams(
            dimension_semantics=("parallel",), vmem_limit_bytes=VMEM_LIMIT),
        name="out_proj_ln",
    )(o_r, o_g, w_out, x2, g, b)


UP_TM = 1024
UP_TF = 512
UP_SUB = 1024
HALO = 16


def _up_conv_kernel(x_ref, halo_ref, wv_ref, wg_ref, cv_ref, cg_ref, bv_ref, bg_ref,
                    o_ref, lhs_ref, uv_ref, ug_ref, *, tiles_per_seq):
    i = pl.program_id(0)
    j = pl.program_id(1)

    @pl.when(j == 0)
    def _():
        keep = (i % tiles_per_seq != 0).astype(halo_ref.dtype)
        lhs_ref[:HALO, :] = halo_ref[...] * keep
        lhs_ref[HALO:, :] = x_ref[...]

    def conv(u_ref, r0, c_ref, bias_ref):
        out = bias_ref[...]
        for tap in range(CONV_WIDTH):
            start = r0 + HALO - (CONV_WIDTH - 1) + tap
            out = out + c_ref[tap:tap + 1, :] * u_ref[start:start + UP_SUB, :]
        return out

    for r in range(UP_TM // UP_SUB):
        r0 = r * UP_SUB
        lo = 0 if r == 0 else r0 + HALO
        rows = slice(lo, r0 + HALO + UP_SUB)
        lhs = lhs_ref[rows, :]
        ug_ref[rows, :] = _dot(lhs, wg_ref[...])
        uv_ref[rows, :] = _dot(lhs, wv_ref[...])
        gate = conv(ug_ref, r0, cg_ref, bg_ref)
        val = conv(uv_ref, r0, cv_ref, bv_ref)
        o_ref[r0:r0 + UP_SUB, :] = (_silu(gate) * val).astype(o_ref.dtype)


def _up_conv(x1b, w_up, w_conv, b_conv, seq):
    m = x1b.shape[0]
    nf = D_FF // UP_TF
    tiles_per_seq = seq // UP_TM
    halo_blocks = UP_TM // HALO
    kern = functools.partial(_up_conv_kernel, tiles_per_seq=tiles_per_seq)
    return pl.pallas_call(
        kern,
        out_shape=jax.ShapeDtypeStruct((m, D_FF), BF16),
        grid=(m // UP_TM, nf),
        in_specs=[
            pl.BlockSpec((UP_TM, D_MODEL), lambda i, j: (i, 0)),
            pl.BlockSpec((HALO, D_MODEL), lambda i, j: (jnp.maximum(i * halo_blocks - 1, 0), 0)),
            pl.BlockSpec((D_MODEL, UP_TF), lambda i, j: (0, j)),
            pl.BlockSpec((D_MODEL, UP_TF), lambda i, j: (0, nf + j)),
            pl.BlockSpec((CONV_WIDTH, UP_TF), lambda i, j: (0, j)),
            pl.BlockSpec((CONV_WIDTH, UP_TF), lambda i, j: (0, nf + j)),
            pl.BlockSpec((1, UP_TF), lambda i, j: (0, j)),
            pl.BlockSpec((1, UP_TF), lambda i, j: (0, nf + j)),
        ],
        out_specs=pl.BlockSpec((UP_TM, UP_TF), lambda i, j: (i, j)),
        scratch_shapes=[
            pltpu.VMEM((UP_TM + HALO, D_MODEL), BF16),
            pltpu.VMEM((UP_TM + HALO, UP_TF), F32),
            pltpu.VMEM((UP_TM + HALO, UP_TF), F32),
        ],
        compiler_params=pltpu.CompilerParams(
            dimension_semantics=("parallel", "arbitrary"), vmem_limit_bytes=VMEM_LIMIT),
        name="up_conv_gate",
    )(x1b, x1b, w_up, w_up, w_conv, w_conv, b_conv, b_conv)


DOWN_TM = 512
DOWN_SUB = 128


def _down_proj_kernel(a_ref, w_ref, x_ref, g_ref, b_ref, y_ref):
    for r in range(DOWN_TM // DOWN_SUB):
        rows = slice(r * DOWN_SUB, (r + 1) * DOWN_SUB)
        acc = _dot(a_ref[rows, :], w_ref[...])
        y_ref[rows, :] = _layer_norm(DEEPNORM_ALPHA * x_ref[rows, :] + acc, g_ref[...], b_ref[...])


def _down_proj(act, w_down, x1, g, b):
    m = x1.shape[0]
    row = lambda i: (i, 0)
    const = lambda i: (0, 0)
    return pl.pallas_call(
        _down_proj_kernel,
        out_shape=jax.ShapeDtypeStruct((m, D_MODEL), F32),
        grid=(m // DOWN_TM,),
        in_specs=[
            pl.BlockSpec((DOWN_TM, D_FF), row),
            pl.BlockSpec((D_FF, D_MODEL), const, pipeline_mode=pl.Buffered(1)),
            pl.BlockSpec((DOWN_TM, D_MODEL), row),
            pl.BlockSpec((1, D_MODEL), const),
            pl.BlockSpec((1, D_MODEL), const),
        ],
        out_specs=pl.BlockSpec((DOWN_TM, D_MODEL), row),
        compiler_params=pltpu.CompilerParams(
            dimension_semantics=("parallel",), vmem_limit_bytes=VMEM_LIMIT),
        name="down_proj_ln",
    )(act, w_down, x1, g, b)


def _rotary_tables(seq):
    half = RET_DK // 2
    inv_freq = ROPE_BASE ** (-jnp.arange(half, dtype=F32) / half)
    ang = jnp.arange(seq, dtype=F32)[:, None] * inv_freq[None, :]
    return jnp.cos(ang), jnp.sin(ang)


def _retention_tables():
    log_gamma = jnp.log1p(-jnp.exp2(-5.0 - jnp.arange(RET_HEADS, dtype=F32)))
    idx = jnp.arange(RET_BLOCK, dtype=F32)
    dist = idx[:, None] - idx[None, :]
    chunk_gap = idx[:, None] // CHUNK - idx[None, :] // CHUNK
    expo = jnp.where(chunk_gap == 0, jnp.abs(dist), dist)
    dmat = jnp.where(chunk_gap >= 0, jnp.exp(log_gamma[:, None, None] * expo), 0.0)
    qdec = jnp.exp(log_gamma[:, None] * (idx + 1.0))
    kdec = jnp.exp(log_gamma[:, None] * (RET_BLOCK - 1.0 - idx))
    sdec = jnp.exp(log_gamma * RET_BLOCK)
    qdec = jnp.broadcast_to(qdec[:, :, None], (RET_HEADS, RET_BLOCK, RET_DV))
    kdec = jnp.broadcast_to(kdec[:, :, None], (RET_HEADS, RET_BLOCK, RET_DV))
    sdec = jnp.broadcast_to(sdec[:, None, None], (RET_HEADS, 1, RET_DV))
    return dmat, qdec, kdec, sdec


def _chunk_tri(n):
    r = jnp.arange(n)
    same = (r[:, None] // CHUNK) == (r[None, :] // CHUNK)
    return (same & (r[:, None] >= r[None, :])).astype(BF16)


def kernel(x, w_in, w_gla_gate, b_gla_gate, g_ret, g_gla, w_out, ln1_g, ln1_b, w_up, w_conv,
           b_conv, w_down, ln2_g, ln2_b):
    batch, seq, d = x.shape
    m = batch * seq
    cos, sin = _rotary_tables(seq)
    dmat, qdec, kdec, sdec = _retention_tables()
    tri = _chunk_tri(TRI)
    x2 = x.reshape(m, d)
    for layer in range(DEPTH):
        w_in_l = w_in[layer]
        w_main = w_in_l.astype(BF16)
        w_a = jnp.pad(w_in_l[:, H_WIDTH:], ((0, 0), (0, LANES - GLA_GATE_RANK))).astype(BF16)
        w_gate = jnp.pad(w_gla_gate[layer], ((0, LANES - GLA_GATE_RANK), (0, 0))).astype(BF16)
        h, b_cum = _in_proj(x2, w_main, cos, sin, w_a, w_gate,
                            b_gla_gate[layer].astype(F32)[None, :], tri, seq)
        o_r, o_g = _mixer(h, b_cum, dmat, qdec, kdec, sdec,
                          g_ret[layer].astype(F32)[None, :], g_gla[layer].astype(F32)[None, :],
                          batch, seq)
        x1, x1b = _out_proj(o_r, o_g, w_out[layer].astype(BF16), x2,
                            ln1_g[layer][None, :], ln1_b[layer][None, :])
        act = _up_conv(x1b, w_up[layer].astype(BF16), w_conv[layer], b_conv[layer][None, :], seq)
        x2 = _down_proj(act, w_down[layer].astype(BF16), x1,
                        ln2_g[layer][None, :], ln2_b[layer][None, :])
    return x2.reshape(batch, seq, d)
```

```python
import functools

import jax
import jax.numpy as jnp
from jax import lax
from jax.experimental import pallas as pl
from jax.experimental.pallas import tpu as pltpu

D_MODEL = 2048
CHUNK = 64
RET_HEADS = 4
RET_DK = 256
RET_DV = 256
GLA_HEADS = 4
GLA_DK = 128
GLA_DV = 256
GLA_GATE_RANK = 16
GLA_GATE_TAU = 16.0
ROPE_BASE = 10000.0
D_FF = 5632
CONV_WIDTH = 3
LN_EPS = 1e-5
DEPTH = 1
DEEPNORM_ALPHA = (2.0 * DEPTH) ** 0.25

RET_QK = RET_HEADS * RET_DK
RET_V = RET_HEADS * RET_DV
GLA_QK = GLA_HEADS * GLA_DK
GLA_V = GLA_HEADS * GLA_DV
H_WIDTH = 2 * RET_QK + 2 * RET_V + 2 * GLA_QK + 2 * GLA_V

LANES = 128
VMEM_LIMIT = 56 * 1024 * 1024

F32 = jnp.float32
BF16 = jnp.bfloat16


def _dot(a, b):
    return jnp.dot(a, b, preferred_element_type=F32)


def _dot_nt(a, b):
    return lax.dot_general(a, b, (((1,), (1,)), ((), ())), preferred_element_type=F32)


def _dot_tn(a, b):
    return lax.dot_general(a, b, (((0,), (0,)), ((), ())), preferred_element_type=F32)


def _silu(z):
    return z / (1.0 + jnp.exp(-z))


def _layer_norm(y, g, b):
    mu = jnp.mean(y, axis=-1, keepdims=True)
    yc = y - mu
    var = jnp.mean(yc * yc, axis=-1, keepdims=True)
    return yc * lax.rsqrt(var + LN_EPS) * g + b


IN_TM = 1024
IN_TN = 1024
IN_SUB = 256
TRI = IN_SUB


def _in_proj_kernel(x_ref, w_ref, cos_ref, sin_ref, wa_ref, wg_ref, bias_ref, tri_ref,
                    o_ref, b_ref, xb_ref):
    j = pl.program_id(1)

    def gate(xb, rows):
        half = D_MODEL // 2
        a = _dot(xb[:, :half], wa_ref[:half, :]) + _dot(xb[:, half:], wa_ref[half:, :])
        z = _dot(a.astype(BF16), wg_ref[...]) + bias_ref[...]
        log_a = (jnp.minimum(z, 0.0) - jnp.log1p(jnp.exp(-jnp.abs(z)))) * (1.0 / GLA_GATE_TAU)
        hi = log_a.astype(BF16)
        lo = (log_a - hi.astype(F32)).astype(BF16)
        b_ref[rows, :] = _dot(tri_ref[...], hi) + _dot(tri_ref[...], lo)

    def rotary(acc, rows, scale):
        cos = cos_ref[rows, :]
        sin = sin_ref[rows, :]
        for hd in range(IN_TN // RET_DK):
            lo = hd * RET_DK
            x1 = acc[:, lo:lo + LANES]
            x2 = acc[:, lo + LANES:lo + 2 * LANES]
            o_ref[rows, lo:lo + LANES] = ((x1 * cos - x2 * sin) * scale).astype(o_ref.dtype)
            o_ref[rows, lo + LANES:lo + 2 * LANES] = ((x1 * sin + x2 * cos) * scale).astype(o_ref.dtype)

    def gla_qk(acc, rows):
        o_ref[rows, :GLA_QK] = (acc[:, :GLA_QK] * (GLA_DK ** -0.5)).astype(o_ref.dtype)
        o_ref[rows, GLA_QK:] = acc[:, GLA_QK:].astype(o_ref.dtype)

    def plain(acc, rows):
        o_ref[rows, :] = acc.astype(o_ref.dtype)

    def project(epilogue):
        for r in range(IN_TM // IN_SUB):
            rows = slice(r * IN_SUB, (r + 1) * IN_SUB)
            epilogue(_dot(xb_ref[rows, :], w_ref[...]), rows)

    @pl.when(j == 0)
    def _():
        for r in range(IN_TM // IN_SUB):
            rows = slice(r * IN_SUB, (r + 1) * IN_SUB)
            xb = x_ref[rows, :].astype(xb_ref.dtype)
            xb_ref[rows, :] = xb
            rotary(_dot(xb, w_ref[...]), rows, 1.0)
            gate(xb, rows)

    @pl.when(j == 1)
    def _():
        project(functools.partial(rotary, scale=RET_DK ** -0.5))

    @pl.when(j == 4)
    def _():
        project(gla_qk)

    @pl.when((j == 2) | (j == 3) | (j >= 5))
    def _():
        project(plain)


def _in_proj(x2, w_main, cos, sin, w_a, w_gate, bias, tri, seq):
    m = x2.shape[0]
    nseq = seq // IN_TM
    const = lambda i, j: (0, 0)
    return pl.pallas_call(
        _in_proj_kernel,
        out_shape=(jax.ShapeDtypeStruct((m, H_WIDTH), BF16), jax.ShapeDtypeStruct((m, GLA_QK), F32)),
        grid=(m // IN_TM, H_WIDTH // IN_TN),
        in_specs=[
            pl.BlockSpec((IN_TM, D_MODEL), lambda i, j: (i, 0)),
            pl.BlockSpec((D_MODEL, IN_TN), lambda i, j: (0, j)),
            pl.BlockSpec((IN_TM, LANES), lambda i, j: (i % nseq, 0)),
            pl.BlockSpec((IN_TM, LANES), lambda i, j: (i % nseq, 0)),
            pl.BlockSpec((D_MODEL, LANES), const),
            pl.BlockSpec((LANES, GLA_QK), const),
            pl.BlockSpec((1, GLA_QK), const),
            pl.BlockSpec((TRI, TRI), const),
        ],
        out_specs=(pl.BlockSpec((IN_TM, IN_TN), lambda i, j: (i, j)),
                   pl.BlockSpec((IN_TM, GLA_QK), lambda i, j: (i, 0))),
        scratch_shapes=[pltpu.VMEM((IN_TM, D_MODEL), BF16)],
        compiler_params=pltpu.CompilerParams(
            dimension_semantics=("parallel", "arbitrary"), vmem_limit_bytes=VMEM_LIMIT),
        name="in_proj",
    )(x2, w_main, cos, sin, w_a, w_gate, bias, tri)


MIX_T = 1024
RET_BLOCK = 256


def _mixer_kernel(qr_ref, kr_ref, vr_ref, zr_ref, qg_ref, kg_ref, vg_ref, zg_ref, bg_ref,
                  dmat_ref, qdec_ref, kdec_ref, sdec_ref, gr_ref, gg_ref,
                  or_ref, og_ref, sr_ref, sg_ref):
    @pl.when(pl.program_id(2) == 0)
    def _():
        sr_ref[...] = jnp.zeros_like(sr_ref)
        sg_ref[...] = jnp.zeros_like(sg_ref)

    g_ret = gr_ref[...]
    g_gla = gg_ref[...]
    row = lax.broadcasted_iota(jnp.int32, (CHUNK, CHUNK), 0)
    col = lax.broadcasted_iota(jnp.int32, (CHUNK, CHUNK), 1)
    causal = row >= col

    def ret_block(r0):
        rows = slice(r0, r0 + RET_BLOCK)
        q = qr_ref[rows, :]
        k = kr_ref[rows, :]
        v = vr_ref[rows, :]
        p = (_dot_nt(q, k) * dmat_ref[0]).astype(BF16)
        s_old = sr_ref[...]
        o = _dot(p, v) + qdec_ref[0] * _dot(q, s_old.astype(BF16))
        vk = (v.astype(F32) * kdec_ref[0]).astype(BF16)
        sr_ref[...] = s_old * sdec_ref[0] + _dot_tn(k, vk)
        o = o - jnp.mean(o, axis=-1, keepdims=True)
        o = o * lax.rsqrt(jnp.mean(o * o, axis=-1, keepdims=True) + LN_EPS)
        or_ref[rows, :] = (o * g_ret * _silu(zr_ref[rows, :].astype(F32))).astype(or_ref.dtype)

    def gla_chunk(r0):
        rows = slice(r0, r0 + CHUNK)
        b = bg_ref[rows, :]
        b_mid = b[CHUNK // 2:CHUNK // 2 + 1, :]
        b_last = b[CHUNK - 1:CHUNK, :]
        bc = b - b_mid
        e_pos = jnp.exp(bc)
        e_neg = jnp.exp(-bc)
        q = qg_ref[rows, :].astype(F32)
        k = kg_ref[rows, :].astype(F32)
        v = vg_ref[rows, :]
        a_causal = _dot_nt((q * e_pos).astype(BF16), (k * e_neg).astype(BF16))
        a_anti = _dot_nt((q * e_neg).astype(BF16), (k * e_pos).astype(BF16))
        p = jnp.where(causal, a_causal, a_anti).astype(BF16)
        st_old = sg_ref[...]
        o = _dot(p, v) + _dot_nt((q * jnp.exp(b)).astype(BF16), st_old.astype(BF16))
        kb = (k * jnp.exp(b_last - b)).astype(BF16)
        sg_ref[...] = st_old * jnp.exp(b_last) + _dot_tn(v, kb)
        o = o * lax.rsqrt(jnp.mean(o * o, axis=-1, keepdims=True) + LN_EPS)
        og_ref[rows, :] = (o * g_gla * _silu(zg_ref[rows, :].astype(F32))).astype(og_ref.dtype)

    for r0 in range(0, MIX_T, RET_BLOCK):
        ret_block(r0)
        for c0 in range(r0, r0 + RET_BLOCK, CHUNK):
            gla_chunk(c0)


def _mixer(h, b_cum, dmat, qdec, kdec, sdec, g_ret, g_gla, batch, seq):
    m = h.shape[0]
    nt = seq // MIX_T
    rd = RET_DV
    c_kr = RET_QK // rd
    c_vr = 2 * RET_QK // rd
    c_zr = (2 * RET_QK + RET_V) // rd
    g0 = 2 * RET_QK + 2 * RET_V
    c_qg = g0 // GLA_DK
    c_kg = (g0 + GLA_QK) // GLA_DK
    c_vg = (g0 + 2 * GLA_QK) // rd
    c_zg = (g0 + 2 * GLA_QK + GLA_V) // rd

    def rows(bi, hi, ti):
        return bi * nt + ti

    def hspec(width, first):
        return pl.BlockSpec((MIX_T, width), lambda bi, hi, ti: (rows(bi, hi, ti), first + hi))

    def head_table(shape):
        return pl.BlockSpec((1,) + shape, lambda bi, hi, ti: (hi, 0, 0))

    out_spec = pl.BlockSpec((MIX_T, rd), lambda bi, hi, ti: (rows(bi, hi, ti), hi))
    return pl.pallas_call(
        _mixer_kernel,
        out_shape=(jax.ShapeDtypeStruct((m, RET_V), BF16), jax.ShapeDtypeStruct((m, GLA_V), BF16)),
        grid=(batch, RET_HEADS, nt),
        in_specs=[
            hspec(rd, 0), hspec(rd, c_kr), hspec(rd, c_vr), hspec(rd, c_zr),
            hspec(GLA_DK, c_qg), hspec(GLA_DK, c_kg), hspec(rd, c_vg), hspec(rd, c_zg),
            pl.BlockSpec((MIX_T, GLA_DK), lambda bi, hi, ti: (rows(bi, hi, ti), hi)),
            head_table((RET_BLOCK, RET_BLOCK)), head_table((RET_BLOCK, RET_DV)),
            head_table((RET_BLOCK, RET_DV)), head_table((1, RET_DV)),
            pl.BlockSpec((1, rd), lambda bi, hi, ti: (0, hi)),
            pl.BlockSpec((1, rd), lambda bi, hi, ti: (0, hi)),
        ],
        out_specs=(out_spec, out_spec),
        scratch_shapes=[pltpu.VMEM((RET_DK, RET_DV), F32), pltpu.VMEM((GLA_DV, GLA_DK), F32)],
        compiler_params=pltpu.CompilerParams(
            dimension_semantics=("parallel", "parallel", "arbitrary"), vmem_limit_bytes=VMEM_LIMIT),
        name="mixer",
    )(h, h, h, h, h, h, h, h, b_cum, dmat, qdec, kdec, sdec, g_ret, g_gla)


OUT_TM = 512
OUT_SUB = 128


def _out_proj_kernel(or_ref, og_ref, w_ref, x_ref, g_ref, b_ref, y_ref, yb_ref):
    for r in range(OUT_TM // OUT_SUB):
        rows = slice(r * OUT_SUB, (r + 1) * OUT_SUB)
        acc = _dot(or_ref[rows, :], w_ref[:RET_V, :]) + _dot(og_ref[rows, :], w_ref[RET_V:, :])
        y = _layer_norm(DEEPNORM_ALPHA * x_ref[rows, :] + acc, g_ref[...], b_ref[...])
        y_ref[rows, :] = y
        yb_ref[rows, :] = y.astype(yb_ref.dtype)


def _out_proj(o_r, o_g, w_out, x2, g, b):
    m = x2.shape[0]
    row = lambda i: (i, 0)
    const = lambda i: (0, 0)
    return pl.pallas_call(
        _out_proj_kernel,
        out_shape=(jax.ShapeDtypeStruct((m, D_MODEL), F32), jax.ShapeDtypeStruct((m, D_MODEL), BF16)),
        grid=(m // OUT_TM,),
        in_specs=[
            pl.BlockSpec((OUT_TM, RET_V), row),
            pl.BlockSpec((OUT_TM, GLA_V), row),
            pl.BlockSpec((RET_V + GLA_V, D_MODEL), const),
            pl.BlockSpec((OUT_TM, D_MODEL), row),
            pl.BlockSpec((1, D_MODEL), const),
            pl.BlockSpec((1, D_MODEL), const),
        ],
        out_specs=(pl.BlockSpec((OUT_TM, D_MODEL), row), pl.BlockSpec((OUT_TM, D_MODEL), row)),
        compiler_params=pltpu.CompilerParams(
            dimension_semantics=("parallel",), vmem_limit_bytes=VMEM_LIMIT),
        name="out_proj_ln",
    )(o_r, o_g, w_out, x2, g, b)


UP_TM = 1024
UP_TF = 512
HALO = 16


def _up_conv_kernel(x_ref, halo_ref, wv_ref, wg_ref, cv_ref, cg_ref, bv_ref, bg_ref,
                    o_ref, lhs_ref, uv_ref, ug_ref, *, tiles_per_seq):
    i = pl.program_id(0)
    j = pl.program_id(1)

    @pl.when(j == 0)
    def _():
        keep = (i % tiles_per_seq != 0).astype(halo_ref.dtype)
        lhs_ref[:HALO, :] = halo_ref[...] * keep
        lhs_ref[HALO:, :] = x_ref[...]

    def conv(u_ref, c_ref, bias_ref):
        u = u_ref[...]
        acc = c_ref[0:1, :] * u
        for tap in range(1, CONV_WIDTH):
            acc = c_ref[tap:tap + 1, :] * u + pltpu.roll(acc, 1, axis=0)
        return (acc + bias_ref[...])[HALO:, :]

    lhs = lhs_ref[...]
    ug_ref[...] = _dot(lhs, wg_ref[...])
    uv_ref[...] = _dot(lhs, wv_ref[...])
    gate = conv(ug_ref, cg_ref, bg_ref)
    val = conv(uv_ref, cv_ref, bv_ref)
    o_ref[...] = (_silu(gate) * val).astype(o_ref.dtype)


def _up_conv(x1b, w_up, w_conv, b_conv, seq):
    m = x1b.shape[0]
    nf = D_FF // UP_TF
    tiles_per_seq = seq // UP_TM
    halo_blocks = UP_TM // HALO
    kern = functools.partial(_up_conv_kernel, tiles_per_seq=tiles_per_seq)
    return pl.pallas_call(
        kern,
        out_shape=jax.ShapeDtypeStruct((m, D_FF), BF16),
        grid=(m // UP_TM, nf),
        in_specs=[
            pl.BlockSpec((UP_TM, D_MODEL), lambda i, j: (i, 0)),
            pl.BlockSpec((HALO, D_MODEL), lambda i, j: (jnp.maximum(i * halo_blocks - 1, 0), 0)),
            pl.BlockSpec((D_MODEL, UP_TF), lambda i, j: (0, j)),
            pl.BlockSpec((D_MODEL, UP_TF), lambda i, j: (0, nf + j)),
            pl.BlockSpec((CONV_WIDTH, UP_TF), lambda i, j: (0, j)),
            pl.BlockSpec((CONV_WIDTH, UP_TF), lambda i, j: (0, nf + j)),
            pl.BlockSpec((1, UP_TF), lambda i, j: (0, j)),
            pl.BlockSpec((1, UP_TF), lambda i, j: (0, nf + j)),
        ],
        out_specs=pl.BlockSpec((UP_TM, UP_TF), lambda i, j: (i, j)),
        scratch_shapes=[
            pltpu.VMEM((UP_TM + HALO, D_MODEL), BF16),
            pltpu.VMEM((UP_TM + HALO, UP_TF), F32),
            pltpu.VMEM((UP_TM + HALO, UP_TF), F32),
        ],
        compiler_params=pltpu.CompilerParams(
            dimension_semantics=("parallel", "arbitrary"), vmem_limit_bytes=VMEM_LIMIT),
        name="up_conv_gate",
    )(x1b, x1b, w_up, w_up, w_conv, w_conv, b_conv, b_conv)


DOWN_TM = 512
DOWN_SUB = 128


def _down_proj_kernel(a_ref, w_ref, x_ref, g_ref, b_ref, y_ref):
    for r in range(DOWN_TM // DOWN_SUB):
        rows = slice(r * DOWN_SUB, (r + 1) * DOWN_SUB)
        acc = _dot(a_ref[rows, :], w_ref[...])
        y_ref[rows, :] = _layer_norm(DEEPNORM_ALPHA * x_ref[rows, :] + acc, g_ref[...], b_ref[...])


def _down_proj(act, w_down, x1, g, b):
    m = x1.shape[0]
    row = lambda i: (i, 0)
    const = lambda i: (0, 0)
    return pl.pallas_call(
        _down_proj_kernel,
        out_shape=jax.ShapeDtypeStruct((m, D_MODEL), F32),
        grid=(m // DOWN_TM,),
        in_specs=[
            pl.BlockSpec((DOWN_TM, D_FF), row),
            pl.BlockSpec((D_FF, D_MODEL), const, pipeline_mode=pl.Buffered(1)),
            pl.BlockSpec((DOWN_TM, D_MODEL), row),
            pl.BlockSpec((1, D_MODEL), const),
            pl.BlockSpec((1, D_MODEL), const),
        ],
        out_specs=pl.BlockSpec((DOWN_TM, D_MODEL), row),
        compiler_params=pltpu.CompilerParams(
            dimension_semantics=("parallel",), vmem_limit_bytes=VMEM_LIMIT),
        name="down_proj_ln",
    )(act, w_down, x1, g, b)


def _rotary_tables(seq):
    half = RET_DK // 2
    inv_freq = ROPE_BASE ** (-jnp.arange(half, dtype=F32) / half)
    ang = jnp.arange(seq, dtype=F32)[:, None] * inv_freq[None, :]
    return jnp.cos(ang), jnp.sin(ang)


def _retention_tables():
    log_gamma = jnp.log1p(-jnp.exp2(-5.0 - jnp.arange(RET_HEADS, dtype=F32)))
    idx = jnp.arange(RET_BLOCK, dtype=F32)
    dist = idx[:, None] - idx[None, :]
    chunk_gap = idx[:, None] // CHUNK - idx[None, :] // CHUNK
    expo = jnp.where(chunk_gap == 0, jnp.abs(dist), dist)
    dmat = jnp.where(chunk_gap >= 0, jnp.exp(log_gamma[:, None, None] * expo), 0.0)
    qdec = jnp.exp(log_gamma[:, None] * (idx + 1.0))
    kdec = jnp.exp(log_gamma[:, None] * (RET_BLOCK - 1.0 - idx))
    sdec = jnp.exp(log_gamma * RET_BLOCK)
    qdec = jnp.broadcast_to(qdec[:, :, None], (RET_HEADS, RET_BLOCK, RET_DV))
    kdec = jnp.broadcast_to(kdec[:, :, None], (RET_HEADS, RET_BLOCK, RET_DV))
    sdec = jnp.broadcast_to(sdec[:, None, None], (RET_HEADS, 1, RET_DV))
    return dmat, qdec, kdec, sdec


def _chunk_tri(n):
    r = jnp.arange(n)
    same = (r[:, None] // CHUNK) == (r[None, :] // CHUNK)
    return (same & (r[:, None] >= r[None, :])).astype(BF16)


def kernel(x, w_in, w_gla_gate, b_gla_gate, g_ret, g_gla, w_out, ln1_g, ln1_b, w_up, w_conv,
           b_conv, w_down, ln2_g, ln2_b):
    batch, seq, d = x.shape
    m = batch * seq
    cos, sin = _rotary_tables(seq)
    dmat, qdec, kdec, sdec = _retention_tables()
    tri = _chunk_tri(TRI)
    x2 = x.reshape(m, d)
    for layer in range(DEPTH):
        w_in_l = w_in[layer]
        w_main = w_in_l.astype(BF16)
        w_a = jnp.pad(w_in_l[:, H_WIDTH:], ((0, 0), (0, LANES - GLA_GATE_RANK))).astype(BF16)
        w_gate = jnp.pad(w_gla_gate[layer], ((0, LANES - GLA_GATE_RANK), (0, 0))).astype(BF16)
        h, b_cum = _in_proj(x2, w_main, cos, sin, w_a, w_gate,
                            b_gla_gate[layer].astype(F32)[None, :], tri, seq)
        o_r, o_g = _mixer(h, b_cum, dmat, qdec, kdec, sdec,
                          g_ret[layer].astype(F32)[None, :], g_gla[layer].astype(F32)[None, :],
                          batch, seq)
        x1, x1b = _out_proj(o_r, o_g, w_out[layer].astype(BF16), x2,
                            ln1_g[layer][None, :], ln1_b[layer][None, :])
        act = _up_conv(x1b, w_up[layer].astype(BF16), w_conv[layer], b_conv[layer][None, :], seq)
        x2 = _down_proj(act, w_down[layer].astype(BF16), x1,
                        ln2_g[layer][None, :], ln2_b[layer][None, :])
    return x2.reshape(batch, seq, d)
```

```python
import functools

import jax
import jax.numpy as jnp
from jax import lax
from jax.experimental import pallas as pl
from jax.experimental.pallas import tpu as pltpu

D_MODEL = 2048
CHUNK = 64
RET_HEADS = 4
RET_DK = 256
RET_DV = 256
GLA_HEADS = 4
GLA_DK = 128
GLA_DV = 256
GLA_GATE_RANK = 16
GLA_GATE_TAU = 16.0
ROPE_BASE = 10000.0
D_FF = 5632
CONV_WIDTH = 3
LN_EPS = 1e-5
DEPTH = 1
DEEPNORM_ALPHA = (2.0 * DEPTH) ** 0.25

RET_QK = RET_HEADS * RET_DK
RET_V = RET_HEADS * RET_DV
GLA_QK = GLA_HEADS * GLA_DK
GLA_V = GLA_HEADS * GLA_DV
H_WIDTH = 2 * RET_QK + 2 * RET_V + 2 * GLA_QK + 2 * GLA_V

LANES = 128
VMEM_LIMIT = 56 * 1024 * 1024

F32 = jnp.float32
BF16 = jnp.bfloat16


def _dot(a, b):
    return jnp.dot(a, b, preferred_element_type=F32)


def _dot_nt(a, b):
    return lax.dot_general(a, b, (((1,), (1,)), ((), ())), preferred_element_type=F32)


def _dot_tn(a, b):
    return lax.dot_general(a, b, (((0,), (0,)), ((), ())), preferred_element_type=F32)


def _silu(z):
    return z / (1.0 + jnp.exp(-z))


def _layer_norm(y, g, b):
    mu = jnp.mean(y, axis=-1, keepdims=True)
    yc = y - mu
    var = jnp.mean(yc * yc, axis=-1, keepdims=True)
    return yc * lax.rsqrt(var + LN_EPS) * g + b


IN_TM = 1024
IN_TN = 1024
IN_SUB = 256
TRI = IN_SUB


def _in_proj_kernel(x_ref, w_ref, cos_ref, sin_ref, wa_ref, wg_ref, bias_ref, tri_ref,
                    o_ref, b_ref, xb_ref):
    j = pl.program_id(1)

    def gate(xb, rows):
        half = D_MODEL // 2
        a = _dot(xb[:, :half], wa_ref[:half, :]) + _dot(xb[:, half:], wa_ref[half:, :])
        z = _dot(a.astype(BF16), wg_ref[...]) + bias_ref[...]
        log_a = (jnp.minimum(z, 0.0) - jnp.log1p(jnp.exp(-jnp.abs(z)))) * (1.0 / GLA_GATE_TAU)
        hi = log_a.astype(BF16)
        lo = (log_a - hi.astype(F32)).astype(BF16)
        b_ref[rows, :] = _dot(tri_ref[...], hi) + _dot(tri_ref[...], lo)

    def rotary(acc, rows, scale):
        cos = cos_ref[rows, :]
        sin = sin_ref[rows, :]
        for hd in range(IN_TN // RET_DK):
            lo = hd * RET_DK
            x1 = acc[:, lo:lo + LANES]
            x2 = acc[:, lo + LANES:lo + 2 * LANES]
            o_ref[rows, lo:lo + LANES] = ((x1 * cos - x2 * sin) * scale).astype(o_ref.dtype)
            o_ref[rows, lo + LANES:lo + 2 * LANES] = ((x1 * sin + x2 * cos) * scale).astype(o_ref.dtype)

    def gla_qk(acc, rows):
        o_ref[rows, :GLA_QK] = (acc[:, :GLA_QK] * (GLA_DK ** -0.5)).astype(o_ref.dtype)
        o_ref[rows, GLA_QK:] = acc[:, GLA_QK:].astype(o_ref.dtype)

    def plain(acc, rows):
        o_ref[rows, :] = acc.astype(o_ref.dtype)

    def project(epilogue):
        for r in range(IN_TM // IN_SUB):
            rows = slice(r * IN_SUB, (r + 1) * IN_SUB)
            epilogue(_dot(xb_ref[rows, :], w_ref[...]), rows)

    @pl.when(j == 0)
    def _():
        for r in range(IN_TM // IN_SUB):
            rows = slice(r * IN_SUB, (r + 1) * IN_SUB)
            xb = x_ref[rows, :].astype(xb_ref.dtype)
            xb_ref[rows, :] = xb
            rotary(_dot(xb, w_ref[...]), rows, 1.0)
            gate(xb, rows)

    @pl.when(j == 1)
    def _():
        project(functools.partial(rotary, scale=RET_DK ** -0.5))

    @pl.when(j == 4)
    def _():
        project(gla_qk)

    @pl.when((j == 2) | (j == 3) | (j >= 5))
    def _():
        project(plain)


def _in_proj(x2, w_main, cos, sin, w_a, w_gate, bias, tri, seq):
    m = x2.shape[0]
    nseq = seq // IN_TM
    const = lambda i, j: (0, 0)
    return pl.pallas_call(
        _in_proj_kernel,
        out_shape=(jax.ShapeDtypeStruct((m, H_WIDTH), BF16), jax.ShapeDtypeStruct((m, GLA_QK), F32)),
        grid=(m // IN_TM, H_WIDTH // IN_TN),
        in_specs=[
            pl.BlockSpec((IN_TM, D_MODEL), lambda i, j: (i, 0)),
            pl.BlockSpec((D_MODEL, IN_TN), lambda i, j: (0, j)),
            pl.BlockSpec((IN_TM, LANES), lambda i, j: (i % nseq, 0)),
            pl.BlockSpec((IN_TM, LANES), lambda i, j: (i % nseq, 0)),
            pl.BlockSpec((D_MODEL, LANES), const),
            pl.BlockSpec((LANES, GLA_QK), const),
            pl.BlockSpec((1, GLA_QK), const),
            pl.BlockSpec((TRI, TRI), const),
        ],
        out_specs=(pl.BlockSpec((IN_TM, IN_TN), lambda i, j: (i, j)),
                   pl.BlockSpec((IN_TM, GLA_QK), lambda i, j: (i, 0))),
        scratch_shapes=[pltpu.VMEM((IN_TM, D_MODEL), BF16)],
        compiler_params=pltpu.CompilerParams(
            dimension_semantics=("parallel", "arbitrary"), vmem_limit_bytes=VMEM_LIMIT),
        name="in_proj",
    )(x2, w_main, cos, sin, w_a, w_gate, bias, tri)


MIX_T = 2048
RET_BLOCK = 256


def _mixer_kernel(qr_ref, kr_ref, vr_ref, zr_ref, qg_ref, kg_ref, vg_ref, zg_ref, bg_ref,
                  dmat_ref, qdec_ref, kdec_ref, sdec_ref, gr_ref, gg_ref,
                  or_ref, og_ref, sr_ref, sg_ref):
    @pl.when(pl.program_id(2) == 0)
    def _():
        sr_ref[...] = jnp.zeros_like(sr_ref)
        sg_ref[...] = jnp.zeros_like(sg_ref)

    g_ret = gr_ref[...]
    g_gla = gg_ref[...]
    row = lax.broadcasted_iota(jnp.int32, (CHUNK, CHUNK), 0)
    col = lax.broadcasted_iota(jnp.int32, (CHUNK, CHUNK), 1)
    causal = row >= col

    def ret_block(r0):
        rows = slice(r0, r0 + RET_BLOCK)
        q = qr_ref[rows, :]
        k = kr_ref[rows, :]
        v = vr_ref[rows, :]
        p = (_dot_nt(q, k) * dmat_ref[0]).astype(BF16)
        s_old = sr_ref[...]
        o = _dot(p, v) + qdec_ref[0] * _dot(q, s_old.astype(BF16))
        vk = (v.astype(F32) * kdec_ref[0]).astype(BF16)
        sr_ref[...] = s_old * sdec_ref[0] + _dot_tn(k, vk)
        o = o - jnp.mean(o, axis=-1, keepdims=True)
        o = o * lax.rsqrt(jnp.mean(o * o, axis=-1, keepdims=True) + LN_EPS)
        or_ref[rows, :] = (o * g_ret * _silu(zr_ref[rows, :].astype(F32))).astype(or_ref.dtype)

    def gla_chunk(r0):
        rows = slice(r0, r0 + CHUNK)
        b = bg_ref[rows, :]
        b_mid = b[CHUNK // 2:CHUNK // 2 + 1, :]
        b_last = b[CHUNK - 1:CHUNK, :]
        bc = b - b_mid
        e_pos = jnp.exp(bc)
        e_neg = jnp.exp(-bc)
        q = qg_ref[rows, :].astype(F32)
        k = kg_ref[rows, :].astype(F32)
        v = vg_ref[rows, :]
        a_causal = _dot_nt((q * e_pos).astype(BF16), (k * e_neg).astype(BF16))
        a_anti = _dot_nt((q * e_neg).astype(BF16), (k * e_pos).astype(BF16))
        p = jnp.where(causal, a_causal, a_anti).astype(BF16)
        st_old = sg_ref[...]
        o = _dot(p, v) + _dot_nt((q * jnp.exp(b)).astype(BF16), st_old.astype(BF16))
        kb = (k * jnp.exp(b_last - b)).astype(BF16)
        sg_ref[...] = st_old * jnp.exp(b_last) + _dot_tn(v, kb)
        o = o * lax.rsqrt(jnp.mean(o * o, axis=-1, keepdims=True) + LN_EPS)
        og_ref[rows, :] = (o * g_gla * _silu(zg_ref[rows, :].astype(F32))).astype(og_ref.dtype)

    for r0 in range(0, MIX_T, RET_BLOCK):
        ret_block(r0)
        for c0 in range(r0, r0 + RET_BLOCK, CHUNK):
            gla_chunk(c0)


def _mixer(h, b_cum, dmat, qdec, kdec, sdec, g_ret, g_gla, batch, seq):
    m = h.shape[0]
    nt = seq // MIX_T
    rd = RET_DV
    c_kr = RET_QK // rd
    c_vr = 2 * RET_QK // rd
    c_zr = (2 * RET_QK + RET_V) // rd
    g0 = 2 * RET_QK + 2 * RET_V
    c_qg = g0 // GLA_DK
    c_kg = (g0 + GLA_QK) // GLA_DK
    c_vg = (g0 + 2 * GLA_QK) // rd
    c_zg = (g0 + 2 * GLA_QK + GLA_V) // rd

    def rows(bi, hi, ti):
        return bi * nt + ti

    def hspec(width, first):
        return pl.BlockSpec((MIX_T, width), lambda bi, hi, ti: (rows(bi, hi, ti), first + hi))

    def head_table(shape):
        return pl.BlockSpec((1,) + shape, lambda bi, hi, ti: (hi, 0, 0))

    out_spec = pl.BlockSpec((MIX_T, rd), lambda bi, hi, ti: (rows(bi, hi, ti), hi))
    return pl.pallas_call(
        _mixer_kernel,
        out_shape=(jax.ShapeDtypeStruct((m, RET_V), BF16), jax.ShapeDtypeStruct((m, GLA_V), BF16)),
        grid=(batch, RET_HEADS, nt),
        in_specs=[
            hspec(rd, 0), hspec(rd, c_kr), hspec(rd, c_vr), hspec(rd, c_zr),
            hspec(GLA_DK, c_qg), hspec(GLA_DK, c_kg), hspec(rd, c_vg), hspec(rd, c_zg),
            pl.BlockSpec((MIX_T, GLA_DK), lambda bi, hi, ti: (rows(bi, hi, ti), hi)),
            head_table((RET_BLOCK, RET_BLOCK)), head_table((RET_BLOCK, RET_DV)),
            head_table((RET_BLOCK, RET_DV)), head_table((1, RET_DV)),
            pl.BlockSpec((1, rd), lambda bi, hi, ti: (0, hi)),
            pl.BlockSpec((1, rd), lambda bi, hi, ti: (0, hi)),
        ],
        out_specs=(out_spec, out_spec),
        scratch_shapes=[pltpu.VMEM((RET_DK, RET_DV), F32), pltpu.VMEM((GLA_DV, GLA_DK), F32)],
        compiler_params=pltpu.CompilerParams(
            dimension_semantics=("parallel", "parallel", "arbitrary"), vmem_limit_bytes=VMEM_LIMIT),
        name="mixer",
    )(h, h, h, h, h, h, h, h, b_cum, dmat, qdec, kdec, sdec, g_ret, g_gla)


OUT_TM = 512
OUT_SUB = 128


def _out_proj_kernel(or_ref, og_ref, w_ref, x_ref, g_ref, b_ref, y_ref, yb_ref):
    for r in range(OUT_TM // OUT_SUB):
        rows = slice(r * OUT_SUB, (r + 1) * OUT_SUB)
        acc = _dot(or_ref[rows, :], w_ref[:RET_V, :]) + _dot(og_ref[rows, :], w_ref[RET_V:, :])
        y = _layer_norm(DEEPNORM_ALPHA * x_ref[rows, :] + acc, g_ref[...], b_ref[...])
        y_ref[rows, :] = y
        yb_ref[rows, :] = y.astype(yb_ref.dtype)


def _out_proj(o_r, o_g, w_out, x2, g, b):
    m = x2.shape[0]
    row = lambda i: (i, 0)
    const = lambda i: (0, 0)
    return pl.pallas_call(
        _out_proj_kernel,
        out_shape=(jax.ShapeDtypeStruct((m, D_MODEL), F32), jax.ShapeDtypeStruct((m, D_MODEL), BF16)),
        grid=(m // OUT_TM,),
        in_specs=[
            pl.BlockSpec((OUT_TM, RET_V), row),
            pl.BlockSpec((OUT_TM, GLA_V), row),
            pl.BlockSpec((RET_V + GLA_V, D_MODEL), const),
            pl.BlockSpec((OUT_TM, D_MODEL), row),
            pl.BlockSpec((1, D_MODEL), const),
            pl.BlockSpec((1, D_MODEL), const),
        ],
        out_specs=(pl.BlockSpec((OUT_TM, D_MODEL), row), pl.BlockSpec((OUT_TM, D_MODEL), row)),
        compiler_params=pltpu.CompilerParams(
            dimension_semantics=("parallel",), vmem_limit_bytes=VMEM_LIMIT),
        name="out_proj_ln",
    )(o_r, o_g, w_out, x2, g, b)


UP_TM = 1024
UP_TF = 512
HALO = 16


def _up_conv_kernel(x_ref, halo_ref, wv_ref, wg_ref, cv_ref, cg_ref, bv_ref, bg_ref,
                    o_ref, lhs_ref, uv_ref, ug_ref, *, tiles_per_seq):
    i = pl.program_id(0)
    j = pl.program_id(1)

    @pl.when(j == 0)
    def _():
        keep = (i % tiles_per_seq != 0).astype(halo_ref.dtype)
        lhs_ref[:HALO, :] = halo_ref[...] * keep
        lhs_ref[HALO:, :] = x_ref[...]

    def conv(u_ref, c_ref, bias_ref):
        u = u_ref[...]
        acc = c_ref[0:1, :] * u
        for tap in range(1, CONV_WIDTH):
            acc = c_ref[tap:tap + 1, :] * u + pltpu.roll(acc, 1, axis=0)
        return (acc + bias_ref[...])[HALO:, :]

    lhs = lhs_ref[...]
    ug_ref[...] = _dot(lhs, wg_ref[...])
    uv_ref[...] = _dot(lhs, wv_ref[...])
    gate = conv(ug_ref, cg_ref, bg_ref)
    val = conv(uv_ref, cv_ref, bv_ref)
    o_ref[...] = (_silu(gate) * val).astype(o_ref.dtype)


def _up_conv(x1b, w_up, w_conv, b_conv, seq):
    m = x1b.shape[0]
    nf = D_FF // UP_TF
    tiles_per_seq = seq // UP_TM
    halo_blocks = UP_TM // HALO
    kern = functools.partial(_up_conv_kernel, tiles_per_seq=tiles_per_seq)
    return pl.pallas_call(
        kern,
        out_shape=jax.ShapeDtypeStruct((m, D_FF), BF16),
        grid=(m // UP_TM, nf),
        in_specs=[
            pl.BlockSpec((UP_TM, D_MODEL), lambda i, j: (i, 0)),
            pl.BlockSpec((HALO, D_MODEL), lambda i, j: (jnp.maximum(i * halo_blocks - 1, 0), 0)),
            pl.BlockSpec((D_MODEL, UP_TF), lambda i, j: (0, j)),
            pl.BlockSpec((D_MODEL, UP_TF), lambda i, j: (0, nf + j)),
            pl.BlockSpec((CONV_WIDTH, UP_TF), lambda i, j: (0, j)),
            pl.BlockSpec((CONV_WIDTH, UP_TF), lambda i, j: (0, nf + j)),
            pl.BlockSpec((1, UP_TF), lambda i, j: (0, j)),
            pl.BlockSpec((1, UP_TF), lambda i, j: (0, nf + j)),
        ],
        out_specs=pl.BlockSpec((UP_TM, UP_TF), lambda i, j: (i, j)),
        scratch_shapes=[
            pltpu.VMEM((UP_TM + HALO, D_MODEL), BF16),
            pltpu.VMEM((UP_TM + HALO, UP_TF), F32),
            pltpu.VMEM((UP_TM + HALO, UP_TF), F32),
        ],
        compiler_params=pltpu.CompilerParams(
            dimension_semantics=("parallel", "arbitrary"), vmem_limit_bytes=VMEM_LIMIT),
        name="up_conv_gate",
    )(x1b, x1b, w_up, w_up, w_conv, w_conv, b_conv, b_conv)


DOWN_TM = 512
DOWN_SUB = 128


def _down_proj_kernel(a_ref, w_ref, x_ref, g_ref, b_ref, y_ref):
    for r in range(DOWN_TM // DOWN_SUB):
        rows = slice(r * DOWN_SUB, (r + 1) * DOWN_SUB)
        acc = _dot(a_ref[rows, :], w_ref[...])
        y_ref[rows, :] = _layer_norm(DEEPNORM_ALPHA * x_ref[rows, :] + acc, g_ref[...], b_ref[...])


def _down_proj(act, w_down, x1, g, b):
    m = x1.shape[0]
    row = lambda i: (i, 0)
    const = lambda i: (0, 0)
    return pl.pallas_call(
        _down_proj_kernel,
        out_shape=jax.ShapeDtypeStruct((m, D_MODEL), F32),
        grid=(m // DOWN_TM,),
        in_specs=[
            pl.BlockSpec((DOWN_TM, D_FF), row),
            pl.BlockSpec((D_FF, D_MODEL), const, pipeline_mode=pl.Buffered(1)),
            pl.BlockSpec((DOWN_TM, D_MODEL), row),
            pl.BlockSpec((1, D_MODEL), const),
            pl.BlockSpec((1, D_MODEL), const),
        ],
        out_specs=pl.BlockSpec((DOWN_TM, D_MODEL), row),
        compiler_params=pltpu.CompilerParams(
            dimension_semantics=("parallel",), vmem_limit_bytes=VMEM_LIMIT),
        name="down_proj_ln",
    )(act, w_down, x1, g, b)


def _rotary_tables(seq):
    half = RET_DK // 2
    inv_freq = ROPE_BASE ** (-jnp.arange(half, dtype=F32) / half)
    ang = jnp.arange(seq, dtype=F32)[:, None] * inv_freq[None, :]
    return jnp.cos(ang), jnp.sin(ang)


def _retention_tables():
    log_gamma = jnp.log1p(-jnp.exp2(-5.0 - jnp.arange(RET_HEADS, dtype=F32)))
    idx = jnp.arange(RET_BLOCK, dtype=F32)
    dist = idx[:, None] - idx[None, :]
    chunk_gap = idx[:, None] // CHUNK - idx[None, :] // CHUNK
    expo = jnp.where(chunk_gap == 0, jnp.abs(dist), dist)
    dmat = jnp.where(chunk_gap >= 0, jnp.exp(log_gamma[:, None, None] * expo), 0.0)
    qdec = jnp.exp(log_gamma[:, None] * (idx + 1.0))
    kdec = jnp.exp(log_gamma[:, None] * (RET_BLOCK - 1.0 - idx))
    sdec = jnp.exp(log_gamma * RET_BLOCK)
    qdec = jnp.broadcast_to(qdec[:, :, None], (RET_HEADS, RET_BLOCK, RET_DV))
    kdec = jnp.broadcast_to(kdec[:, :, None], (RET_HEADS, RET_BLOCK, RET_DV))
    sdec = jnp.broadcast_to(sdec[:, None, None], (RET_HEADS, 1, RET_DV))
    return dmat, qdec, kdec, sdec


def _chunk_tri(n):
    r = jnp.arange(n)
    same = (r[:, None] // CHUNK) == (r[None, :] // CHUNK)
    return (same & (r[:, None] >= r[None, :])).astype(BF16)


def kernel(x, w_in, w_gla_gate, b_gla_gate, g_ret, g_gla, w_out, ln1_g, ln1_b, w_up, w_conv,
           b_conv, w_down, ln2_g, ln2_b):
    batch, seq, d = x.shape
    m = batch * seq
    cos, sin = _rotary_tables(seq)
    dmat, qdec, kdec, sdec = _retention_tables()
    tri = _chunk_tri(TRI)
    x2 = x.reshape(m, d)
    for layer in range(DEPTH):
        w_in_l = w_in[layer]
        w_main = w_in_l.astype(BF16)
        w_a = jnp.pad(w_in_l[:, H_WIDTH:], ((0, 0), (0, LANES - GLA_GATE_RANK))).astype(BF16)
        w_gate = jnp.pad(w_gla_gate[layer], ((0, LANES - GLA_GATE_RANK), (0, 0))).astype(BF16)
        h, b_cum = _in_proj(x2, w_main, cos, sin, w_a, w_gate,
                            b_gla_gate[layer].astype(F32)[None, :], tri, seq)
        o_r, o_g = _mixer(h, b_cum, dmat, qdec, kdec, sdec,
                          g_ret[layer].astype(F32)[None, :], g_gla[layer].astype(F32)[None, :],
                          batch, seq)
        x1, x1b = _out_proj(o_r, o_g, w_out[layer].astype(BF16), x2,
                            ln1_g[layer][None, :], ln1_b[layer][None, :])
        act = _up_conv(x1b, w_up[layer].astype(BF16), w_conv[layer], b_conv[layer][None, :], seq)
        x2 = _down_proj(act, w_down[layer].astype(BF16), x1,
                        ln2_g[layer][None, :], ln2_b[layer][None, :])
    return x2.reshape(batch, seq, d)
```

```python
import functools

import jax
import jax.numpy as jnp
import numpy as np
from jax import lax
from jax.experimental import pallas as pl
from jax.experimental.pallas import tpu as pltpu

D_MODEL = 2048
CHUNK = 64
RET_HEADS = 4
RET_DK = 256
RET_DV = 256
GLA_HEADS = 4
GLA_DK = 128
GLA_DV = 256
GLA_GATE_RANK = 16
GLA_GATE_TAU = 16.0
ROPE_BASE = 10000.0
D_FF = 5632
CONV_WIDTH = 3
LN_EPS = 1e-5
DEPTH = 1
DEEPNORM_ALPHA = (2.0 * DEPTH) ** 0.25

RET_QK = RET_HEADS * RET_DK
RET_V = RET_HEADS * RET_DV
GLA_QK = GLA_HEADS * GLA_DK
GLA_V = GLA_HEADS * GLA_DV
H_WIDTH = 2 * RET_QK + 2 * RET_V + 2 * GLA_QK + 2 * GLA_V

LANES = 128
VMEM_LIMIT = 56 * 1024 * 1024

F32 = jnp.float32
BF16 = jnp.bfloat16


def _dot(a, b):
    return jnp.dot(a, b, preferred_element_type=F32)


def _dot_nt(a, b):
    return lax.dot_general(a, b, (((1,), (1,)), ((), ())), preferred_element_type=F32)


def _dot_tn(a, b):
    return lax.dot_general(a, b, (((0,), (0,)), ((), ())), preferred_element_type=F32)


def _silu(z):
    return z / (1.0 + jnp.exp(-z))


def _layer_norm(y, g, b):
    mu = jnp.mean(y, axis=-1, keepdims=True)
    yc = y - mu
    var = jnp.mean(yc * yc, axis=-1, keepdims=True)
    return yc * lax.rsqrt(var + LN_EPS) * g + b


IN_TM = 1024
IN_TN = 1024
IN_SUB = 256
TRI = IN_SUB


def _in_proj_kernel(x_ref, w_ref, cos_ref, sin_ref, wa_ref, wg_ref, bias_ref, tri_ref,
                    o_ref, b_ref, xb_ref):
    j = pl.program_id(1)

    def gate(xb, rows):
        half = D_MODEL // 2
        a = _dot(xb[:, :half], wa_ref[:half, :]) + _dot(xb[:, half:], wa_ref[half:, :])
        z = _dot(a.astype(BF16), wg_ref[...]) + bias_ref[...]
        log_a = (jnp.minimum(z, 0.0) - jnp.log1p(jnp.exp(-jnp.abs(z)))) * (1.0 / GLA_GATE_TAU)
        hi = log_a.astype(BF16)
        lo = (log_a - hi.astype(F32)).astype(BF16)
        b_ref[rows, :] = _dot(tri_ref[...], hi) + _dot(tri_ref[...], lo)

    def rotary(acc, rows, scale):
        cos = cos_ref[rows, :]
        sin = sin_ref[rows, :]
        for hd in range(IN_TN // RET_DK):
            lo = hd * RET_DK
            x1 = acc[:, lo:lo + LANES]
            x2 = acc[:, lo + LANES:lo + 2 * LANES]
            o_ref[rows, lo:lo + LANES] = ((x1 * cos - x2 * sin) * scale).astype(o_ref.dtype)
            o_ref[rows, lo + LANES:lo + 2 * LANES] = ((x1 * sin + x2 * cos) * scale).astype(o_ref.dtype)

    def gla_qk(acc, rows):
        o_ref[rows, :GLA_QK] = (acc[:, :GLA_QK] * (GLA_DK ** -0.5)).astype(o_ref.dtype)
        o_ref[rows, GLA_QK:] = acc[:, GLA_QK:].astype(o_ref.dtype)

    def plain(acc, rows):
        o_ref[rows, :] = acc.astype(o_ref.dtype)

    def project(epilogue):
        for r in range(IN_TM // IN_SUB):
            rows = slice(r * IN_SUB, (r + 1) * IN_SUB)
            epilogue(_dot(xb_ref[rows, :], w_ref[...]), rows)

    @pl.when(j == 0)
    def _():
        for r in range(IN_TM // IN_SUB):
            rows = slice(r * IN_SUB, (r + 1) * IN_SUB)
            xb = x_ref[rows, :].astype(xb_ref.dtype)
            xb_ref[rows, :] = xb
            rotary(_dot(xb, w_ref[...]), rows, 1.0)
            gate(xb, rows)

    @pl.when(j == 1)
    def _():
        project(functools.partial(rotary, scale=RET_DK ** -0.5))

    @pl.when(j == 4)
    def _():
        project(gla_qk)

    @pl.when((j == 2) | (j == 3) | (j >= 5))
    def _():
        project(plain)


def _in_proj(x2, w_main, cos, sin, w_a, w_gate, bias, tri, seq):
    m = x2.shape[0]
    nseq = seq // IN_TM
    const = lambda i, j: (0, 0)
    return pl.pallas_call(
        _in_proj_kernel,
        out_shape=(jax.ShapeDtypeStruct((m, H_WIDTH), BF16), jax.ShapeDtypeStruct((m, GLA_QK), F32)),
        grid=(m // IN_TM, H_WIDTH // IN_TN),
        in_specs=[
            pl.BlockSpec((IN_TM, D_MODEL), lambda i, j: (i, 0)),
            pl.BlockSpec((D_MODEL, IN_TN), lambda i, j: (0, j)),
            pl.BlockSpec((IN_TM, LANES), lambda i, j: (i % nseq, 0)),
            pl.BlockSpec((IN_TM, LANES), lambda i, j: (i % nseq, 0)),
            pl.BlockSpec((D_MODEL, LANES), const),
            pl.BlockSpec((LANES, GLA_QK), const),
            pl.BlockSpec((1, GLA_QK), const),
            pl.BlockSpec((TRI, TRI), const),
        ],
        out_specs=(pl.BlockSpec((IN_TM, IN_TN), lambda i, j: (i, j)),
                   pl.BlockSpec((IN_TM, GLA_QK), lambda i, j: (i, 0))),
        scratch_shapes=[pltpu.VMEM((IN_TM, D_MODEL), BF16)],
        compiler_params=pltpu.CompilerParams(
            dimension_semantics=("parallel", "arbitrary"), vmem_limit_bytes=VMEM_LIMIT),
        name="in_proj",
    )(x2, w_main, cos, sin, w_a, w_gate, bias, tri)


MIX_T = 4096
RET_BLOCK = 256


def _mixer_kernel(qr_ref, kr_ref, vr_ref, zr_ref, qg_ref, kg_ref, vg_ref, zg_ref, bg_ref,
                  dmat_ref, qdec_ref, kdec_ref, sdec_ref, gr_ref, gg_ref,
                  or_ref, og_ref, sr_ref, sg_ref):
    @pl.when(pl.program_id(2) == 0)
    def _():
        sr_ref[...] = jnp.zeros_like(sr_ref)
        sg_ref[...] = jnp.zeros_like(sg_ref)

    g_ret = gr_ref[...]
    g_gla = gg_ref[...]
    row = lax.broadcasted_iota(jnp.int32, (CHUNK, CHUNK), 0)
    col = lax.broadcasted_iota(jnp.int32, (CHUNK, CHUNK), 1)
    causal = row >= col

    def ret_block(r0):
        rows = slice(r0, r0 + RET_BLOCK)
        q = qr_ref[rows, :]
        k = kr_ref[rows, :]
        v = vr_ref[rows, :]
        p = (_dot_nt(q, k) * dmat_ref[0]).astype(BF16)
        s_old = sr_ref[...]
        o = _dot(p, v) + qdec_ref[0] * _dot(q, s_old.astype(BF16))
        vk = (v.astype(F32) * kdec_ref[0]).astype(BF16)
        sr_ref[...] = s_old * sdec_ref[0] + _dot_tn(k, vk)
        o = o - jnp.mean(o, axis=-1, keepdims=True)
        o = o * lax.rsqrt(jnp.mean(o * o, axis=-1, keepdims=True) + LN_EPS)
        or_ref[rows, :] = (o * g_ret * _silu(zr_ref[rows, :].astype(F32))).astype(or_ref.dtype)

    def gla_chunk(r0):
        rows = slice(r0, r0 + CHUNK)
        b = bg_ref[rows, :]
        b_mid = b[CHUNK // 2:CHUNK // 2 + 1, :]
        b_last = b[CHUNK - 1:CHUNK, :]
        bc = b - b_mid
        e_pos = jnp.exp(bc)
        e_neg = jnp.exp(-bc)
        q = qg_ref[rows, :].astype(F32)
        k = kg_ref[rows, :].astype(F32)
        v = vg_ref[rows, :]
        a_causal = _dot_nt((q * e_pos).astype(BF16), (k * e_neg).astype(BF16))
        a_anti = _dot_nt((q * e_neg).astype(BF16), (k * e_pos).astype(BF16))
        p = jnp.where(causal, a_causal, a_anti).astype(BF16)
        st_old = sg_ref[...]
        o = _dot(p, v) + _dot_nt((q * jnp.exp(b)).astype(BF16), st_old.astype(BF16))
        kb = (k * jnp.exp(b_last - b)).astype(BF16)
        sg_ref[...] = st_old * jnp.exp(b_last) + _dot_tn(v, kb)
        o = o * lax.rsqrt(jnp.mean(o * o, axis=-1, keepdims=True) + LN_EPS)
        og_ref[rows, :] = (o * g_gla * _silu(zg_ref[rows, :].astype(F32))).astype(og_ref.dtype)

    for r0 in range(0, MIX_T, RET_BLOCK):
        ret_block(r0)
        for c0 in range(r0, r0 + RET_BLOCK, CHUNK):
            gla_chunk(c0)


def _mixer(h, b_cum, dmat, qdec, kdec, sdec, g_ret, g_gla, batch, seq):
    m = h.shape[0]
    nt = seq // MIX_T
    rd = RET_DV
    c_kr = RET_QK // rd
    c_vr = 2 * RET_QK // rd
    c_zr = (2 * RET_QK + RET_V) // rd
    g0 = 2 * RET_QK + 2 * RET_V
    c_qg = g0 // GLA_DK
    c_kg = (g0 + GLA_QK) // GLA_DK
    c_vg = (g0 + 2 * GLA_QK) // rd
    c_zg = (g0 + 2 * GLA_QK + GLA_V) // rd

    def rows(bi, hi, ti):
        return bi * nt + ti

    def hspec(width, first):
        return pl.BlockSpec((MIX_T, width), lambda bi, hi, ti: (rows(bi, hi, ti), first + hi))

    def head_table(shape):
        return pl.BlockSpec((1,) + shape, lambda bi, hi, ti: (hi, 0, 0))

    out_spec = pl.BlockSpec((MIX_T, rd), lambda bi, hi, ti: (rows(bi, hi, ti), hi))
    return pl.pallas_call(
        _mixer_kernel,
        out_shape=(jax.ShapeDtypeStruct((m, RET_V), BF16), jax.ShapeDtypeStruct((m, GLA_V), BF16)),
        grid=(batch, RET_HEADS, nt),
        in_specs=[
            hspec(rd, 0), hspec(rd, c_kr), hspec(rd, c_vr), hspec(rd, c_zr),
            hspec(GLA_DK, c_qg), hspec(GLA_DK, c_kg), hspec(rd, c_vg), hspec(rd, c_zg),
            pl.BlockSpec((MIX_T, GLA_DK), lambda bi, hi, ti: (rows(bi, hi, ti), hi)),
            head_table((RET_BLOCK, RET_BLOCK)), head_table((RET_BLOCK, RET_DV)),
            head_table((RET_BLOCK, RET_DV)), head_table((1, RET_DV)),
            pl.BlockSpec((1, rd), lambda bi, hi, ti: (0, hi)),
            pl.BlockSpec((1, rd), lambda bi, hi, ti: (0, hi)),
        ],
        out_specs=(out_spec, out_spec),
        scratch_shapes=[pltpu.VMEM((RET_DK, RET_DV), F32), pltpu.VMEM((GLA_DV, GLA_DK), F32)],
        compiler_params=pltpu.CompilerParams(
            dimension_semantics=("parallel", "parallel", "arbitrary"), vmem_limit_bytes=VMEM_LIMIT),
        name="mixer",
    )(h, h, h, h, h, h, h, h, b_cum, dmat, qdec, kdec, sdec, g_ret, g_gla)


OUT_TM = 512
OUT_SUB = 128


def _out_proj_kernel(or_ref, og_ref, w_ref, x_ref, g_ref, b_ref, y_ref, yb_ref):
    for r in range(OUT_TM // OUT_SUB):
        rows = slice(r * OUT_SUB, (r + 1) * OUT_SUB)
        acc = _dot(or_ref[rows, :], w_ref[:RET_V, :]) + _dot(og_ref[rows, :], w_ref[RET_V:, :])
        y = _layer_norm(DEEPNORM_ALPHA * x_ref[rows, :] + acc, g_ref[...], b_ref[...])
        y_ref[rows, :] = y
        yb_ref[rows, :] = y.astype(yb_ref.dtype)


def _out_proj(o_r, o_g, w_out, x2, g, b):
    m = x2.shape[0]
    row = lambda i: (i, 0)
    const = lambda i: (0, 0)
    return pl.pallas_call(
        _out_proj_kernel,
        out_shape=(jax.ShapeDtypeStruct((m, D_MODEL), F32), jax.ShapeDtypeStruct((m, D_MODEL), BF16)),
        grid=(m // OUT_TM,),
        in_specs=[
            pl.BlockSpec((OUT_TM, RET_V), row),
            pl.BlockSpec((OUT_TM, GLA_V), row),
            pl.BlockSpec((RET_V + GLA_V, D_MODEL), const),
            pl.BlockSpec((OUT_TM, D_MODEL), row),
            pl.BlockSpec((1, D_MODEL), const),
            pl.BlockSpec((1, D_MODEL), const),
        ],
        out_specs=(pl.BlockSpec((OUT_TM, D_MODEL), row), pl.BlockSpec((OUT_TM, D_MODEL), row)),
        compiler_params=pltpu.CompilerParams(
            dimension_semantics=("parallel",), vmem_limit_bytes=VMEM_LIMIT),
        name="out_proj_ln",
    )(o_r, o_g, w_out, x2, g, b)


UP_TM = 1024
UP_TF = 512
HALO = 16


def _up_conv_kernel(x_ref, halo_ref, wv_ref, wg_ref, cv_ref, cg_ref, bv_ref, bg_ref,
                    o_ref, lhs_ref, uv_ref, ug_ref, *, tiles_per_seq):
    i = pl.program_id(0)
    j = pl.program_id(1)

    @pl.when(j == 0)
    def _():
        keep = (i % tiles_per_seq != 0).astype(halo_ref.dtype)
        lhs_ref[:HALO, :] = halo_ref[...] * keep
        lhs_ref[HALO:, :] = x_ref[...]

    def conv(u_ref, c_ref, bias_ref):
        u = u_ref[...]
        acc = c_ref[0:1, :] * u
        for tap in range(1, CONV_WIDTH):
            acc = c_ref[tap:tap + 1, :] * u + pltpu.roll(acc, 1, axis=0)
        return (acc + bias_ref[...])[HALO:, :]

    lhs = lhs_ref[...]
    ug_ref[...] = _dot(lhs, wg_ref[...])
    uv_ref[...] = _dot(lhs, wv_ref[...])
    gate = conv(ug_ref, cg_ref, bg_ref)
    val = conv(uv_ref, cv_ref, bv_ref)
    o_ref[...] = (_silu(gate) * val).astype(o_ref.dtype)


def _up_conv(x1b, w_up, w_conv, b_conv, seq):
    m = x1b.shape[0]
    nf = D_FF // UP_TF
    tiles_per_seq = seq // UP_TM
    halo_blocks = UP_TM // HALO
    kern = functools.partial(_up_conv_kernel, tiles_per_seq=tiles_per_seq)
    return pl.pallas_call(
        kern,
        out_shape=jax.ShapeDtypeStruct((m, D_FF), BF16),
        grid=(m // UP_TM, nf),
        in_specs=[
            pl.BlockSpec((UP_TM, D_MODEL), lambda i, j: (i, 0)),
            pl.BlockSpec((HALO, D_MODEL), lambda i, j: (jnp.maximum(i * halo_blocks - 1, 0), 0)),
            pl.BlockSpec((D_MODEL, UP_TF), lambda i, j: (0, j)),
            pl.BlockSpec((D_MODEL, UP_TF), lambda i, j: (0, nf + j)),
            pl.BlockSpec((CONV_WIDTH, UP_TF), lambda i, j: (0, j)),
            pl.BlockSpec((CONV_WIDTH, UP_TF), lambda i, j: (0, nf + j)),
            pl.BlockSpec((1, UP_TF), lambda i, j: (0, j)),
            pl.BlockSpec((1, UP_TF), lambda i, j: (0, nf + j)),
        ],
        out_specs=pl.BlockSpec((UP_TM, UP_TF), lambda i, j: (i, j)),
        scratch_shapes=[
            pltpu.VMEM((UP_TM + HALO, D_MODEL), BF16),
            pltpu.VMEM((UP_TM + HALO, UP_TF), F32),
            pltpu.VMEM((UP_TM + HALO, UP_TF), F32),
        ],
        compiler_params=pltpu.CompilerParams(
            dimension_semantics=("parallel", "arbitrary"), vmem_limit_bytes=VMEM_LIMIT),
        name="up_conv_gate",
    )(x1b, x1b, w_up, w_up, w_conv, w_conv, b_conv, b_conv)


DOWN_TM = 512
DOWN_SUB = 128


def _down_proj_kernel(a_ref, w_ref, x_ref, g_ref, b_ref, y_ref):
    for r in range(DOWN_TM // DOWN_SUB):
        rows = slice(r * DOWN_SUB, (r + 1) * DOWN_SUB)
        acc = _dot(a_ref[rows, :], w_ref[...])
        y_ref[rows, :] = _layer_norm(DEEPNORM_ALPHA * x_ref[rows, :] + acc, g_ref[...], b_ref[...])


def _down_proj(act, w_down, x1, g, b):
    m = x1.shape[0]
    row = lambda i: (i, 0)
    const = lambda i: (0, 0)
    return pl.pallas_call(
        _down_proj_kernel,
        out_shape=jax.ShapeDtypeStruct((m, D_MODEL), F32),
        grid=(m // DOWN_TM,),
        in_specs=[
            pl.BlockSpec((DOWN_TM, D_FF), row),
            pl.BlockSpec((D_FF, D_MODEL), const, pipeline_mode=pl.Buffered(1)),
            pl.BlockSpec((DOWN_TM, D_MODEL), row),
            pl.BlockSpec((1, D_MODEL), const),
            pl.BlockSpec((1, D_MODEL), const),
        ],
        out_specs=pl.BlockSpec((DOWN_TM, D_MODEL), row),
        compiler_params=pltpu.CompilerParams(
            dimension_semantics=("parallel",), vmem_limit_bytes=VMEM_LIMIT),
        name="down_proj_ln",
    )(act, w_down, x1, g, b)


def _rotary_tables(seq):
    half = RET_DK // 2
    inv_freq = ROPE_BASE ** (-np.arange(half, dtype=np.float64) / half)
    ang = np.arange(seq, dtype=np.float64)[:, None] * inv_freq[None, :]
    return jnp.asarray(np.cos(ang), F32), jnp.asarray(np.sin(ang), F32)


def _retention_tables():
    log_gamma = jnp.log1p(-jnp.exp2(-5.0 - jnp.arange(RET_HEADS, dtype=F32)))
    idx = jnp.arange(RET_BLOCK, dtype=F32)
    dist = idx[:, None] - idx[None, :]
    chunk_gap = idx[:, None] // CHUNK - idx[None, :] // CHUNK
    expo = jnp.where(chunk_gap == 0, jnp.abs(dist), dist)
    dmat = jnp.where(chunk_gap >= 0, jnp.exp(log_gamma[:, None, None] * expo), 0.0)
    qdec = jnp.exp(log_gamma[:, None] * (idx + 1.0))
    kdec = jnp.exp(log_gamma[:, None] * (RET_BLOCK - 1.0 - idx))
    sdec = jnp.exp(log_gamma * RET_BLOCK)
    qdec = jnp.broadcast_to(qdec[:, :, None], (RET_HEADS, RET_BLOCK, RET_DV))
    kdec = jnp.broadcast_to(kdec[:, :, None], (RET_HEADS, RET_BLOCK, RET_DV))
    sdec = jnp.broadcast_to(sdec[:, None, None], (RET_HEADS, 1, RET_DV))
    return dmat, qdec, kdec, sdec


def _chunk_tri(n):
    r = jnp.arange(n)
    same = (r[:, None] // CHUNK) == (r[None, :] // CHUNK)
    return (same & (r[:, None] >= r[None, :])).astype(BF16)


def kernel(x, w_in, w_gla_gate, b_gla_gate, g_ret, g_gla, w_out, ln1_g, ln1_b, w_up, w_conv,
           b_conv, w_down, ln2_g, ln2_b):
    batch, seq, d = x.shape
    m = batch * seq
    cos, sin = _rotary_tables(seq)
    dmat, qdec, kdec, sdec = _retention_tables()
    tri = _chunk_tri(TRI)
    x2 = x.reshape(m, d)
    for layer in range(DEPTH):
        w_in_l = w_in[layer]
        w_main = w_in_l.astype(BF16)
        w_a = jnp.pad(w_in_l[:, H_WIDTH:], ((0, 0), (0, LANES - GLA_GATE_RANK))).astype(BF16)
        w_gate = jnp.pad(w_gla_gate[layer], ((0, LANES - GLA_GATE_RANK), (0, 0))).astype(BF16)
        h, b_cum = _in_proj(x2, w_main, cos, sin, w_a, w_gate,
                            b_gla_gate[layer].astype(F32)[None, :], tri, seq)
        o_r, o_g = _mixer(h, b_cum, dmat, qdec, kdec, sdec,
                          g_ret[layer].astype(F32)[None, :], g_gla[layer].astype(F32)[None, :],
                          batch, seq)
        x1, x1b = _out_proj(o_r, o_g, w_out[layer].astype(BF16), x2,
                            ln1_g[layer][None, :], ln1_b[layer][None, :])
        act = _up_conv(x1b, w_up[layer].astype(BF16), w_conv[layer], b_conv[layer][None, :], seq)
        x2 = _down_proj(act, w_down[layer].astype(BF16), x1,
                        ln2_g[layer][None, :], ln2_b[layer][None, :])
    return x2.reshape(batch, seq, d)
```

```python
import functools

import jax
import jax.numpy as jnp
import numpy as np
from jax import lax
from jax.experimental import pallas as pl
from jax.experimental.pallas import tpu as pltpu

D_MODEL = 2048
CHUNK = 64
RET_HEADS = 4
RET_DK = 256
RET_DV = 256
GLA_HEADS = 4
GLA_DK = 128
GLA_DV = 256
GLA_GATE_RANK = 16
GLA_GATE_TAU = 16.0
ROPE_BASE = 10000.0
D_FF = 5632
CONV_WIDTH = 3
LN_EPS = 1e-5
DEPTH = 1
DEEPNORM_ALPHA = (2.0 * DEPTH) ** 0.25

RET_QK = RET_HEADS * RET_DK
RET_V = RET_HEADS * RET_DV
GLA_QK = GLA_HEADS * GLA_DK
GLA_V = GLA_HEADS * GLA_DV
H_WIDTH = 2 * RET_QK + 2 * RET_V + 2 * GLA_QK + 2 * GLA_V

LANES = 128
VMEM_LIMIT = 56 * 1024 * 1024

F32 = jnp.float32
BF16 = jnp.bfloat16


def _dot(a, b):
    return jnp.dot(a, b, preferred_element_type=F32)


def _dot_nt(a, b):
    return lax.dot_general(a, b, (((1,), (1,)), ((), ())), preferred_element_type=F32)


def _dot_tn(a, b):
    return lax.dot_general(a, b, (((0,), (0,)), ((), ())), preferred_element_type=F32)


def _silu(z):
    return z / (1.0 + jnp.exp(-z))


def _layer_norm(y, g, b):
    mu = jnp.mean(y, axis=-1, keepdims=True)
    yc = y - mu
    var = jnp.mean(yc * yc, axis=-1, keepdims=True)
    return yc * lax.rsqrt(var + LN_EPS) * g + b


IN_TM = 1024
IN_TN = 1024
IN_SUB = 256
TRI = IN_SUB


def _in_proj_kernel(x_ref, w_ref, cos_ref, sin_ref, wa_ref, wg_ref, bias_ref, tri_ref,
                    wup_ref, wdn_ref, wout_ref,
                    o_ref, b_ref, wup_b_ref, wdn_b_ref, wout_b_ref, xb_ref):
    j = pl.program_id(1)

    def gate(xb, rows):
        half = D_MODEL // 2
        a = _dot(xb[:, :half], wa_ref[:half, :]) + _dot(xb[:, half:], wa_ref[half:, :])
        z = _dot(a.astype(BF16), wg_ref[...]) + bias_ref[...]
        log_a = (jnp.minimum(z, 0.0) - jnp.log1p(jnp.exp(-jnp.abs(z)))) * (1.0 / GLA_GATE_TAU)
        hi = log_a.astype(BF16)
        lo = (log_a - hi.astype(F32)).astype(BF16)
        b_ref[rows, :] = _dot(tri_ref[...], hi) + _dot(tri_ref[...], lo)

    def rotary(acc, rows, scale):
        cos = cos_ref[rows, :]
        sin = sin_ref[rows, :]
        for hd in range(IN_TN // RET_DK):
            lo = hd * RET_DK
            x1 = acc[:, lo:lo + LANES]
            x2 = acc[:, lo + LANES:lo + 2 * LANES]
            o_ref[rows, lo:lo + LANES] = ((x1 * cos - x2 * sin) * scale).astype(o_ref.dtype)
            o_ref[rows, lo + LANES:lo + 2 * LANES] = ((x1 * sin + x2 * cos) * scale).astype(o_ref.dtype)

    def gla_qk(acc, rows):
        o_ref[rows, :GLA_QK] = (acc[:, :GLA_QK] * (GLA_DK ** -0.5)).astype(o_ref.dtype)
        o_ref[rows, GLA_QK:] = acc[:, GLA_QK:].astype(o_ref.dtype)

    def plain(acc, rows):
        o_ref[rows, :] = acc.astype(o_ref.dtype)

    def project(epilogue):
        for r in range(IN_TM // IN_SUB):
            rows = slice(r * IN_SUB, (r + 1) * IN_SUB)
            epilogue(_dot(xb_ref[rows, :], w_ref[...]), rows)

    @pl.when(j == 0)
    def _():
        for r in range(IN_TM // IN_SUB):
            rows = slice(r * IN_SUB, (r + 1) * IN_SUB)
            xb = x_ref[rows, :].astype(xb_ref.dtype)
            xb_ref[rows, :] = xb
            rotary(_dot(xb, w_ref[...]), rows, 1.0)
            gate(xb, rows)

    @pl.when(j == 1)
    def _():
        project(functools.partial(rotary, scale=RET_DK ** -0.5))

    @pl.when(j == 4)
    def _():
        project(gla_qk)

    def project_and_cast(src_ref, dst_ref):
        project(plain)
        dst_ref[...] = src_ref[...].astype(dst_ref.dtype)

    @pl.when(j == 2)
    def _():
        project_and_cast(wup_ref, wup_b_ref)

    @pl.when(j == 3)
    def _():
        project_and_cast(wdn_ref, wdn_b_ref)

    @pl.when(j == 5)
    def _():
        project_and_cast(wout_ref, wout_b_ref)

    @pl.when(j == 6)
    def _():
        project(plain)


def _in_proj(x2, w_main, cos, sin, w_a, w_gate, bias, tri, side_weights, seq):
    m = x2.shape[0]
    n_rows = m // IN_TM
    nseq = seq // IN_TM
    const = lambda i, j: (0, 0)
    side_specs = [pl.BlockSpec((w.shape[0] // n_rows, w.shape[1]), lambda i, j: (i, 0))
                  for w in side_weights]
    return pl.pallas_call(
        _in_proj_kernel,
        out_shape=(jax.ShapeDtypeStruct((m, H_WIDTH), BF16), jax.ShapeDtypeStruct((m, GLA_QK), F32))
        + tuple(jax.ShapeDtypeStruct(w.shape, BF16) for w in side_weights),
        grid=(n_rows, H_WIDTH // IN_TN),
        in_specs=[
            pl.BlockSpec((IN_TM, D_MODEL), lambda i, j: (i, 0)),
            pl.BlockSpec((D_MODEL, IN_TN), lambda i, j: (0, j)),
            pl.BlockSpec((IN_TM, LANES), lambda i, j: (i % nseq, 0)),
            pl.BlockSpec((IN_TM, LANES), lambda i, j: (i % nseq, 0)),
            pl.BlockSpec((D_MODEL, LANES), const),
            pl.BlockSpec((LANES, GLA_QK), const),
            pl.BlockSpec((1, GLA_QK), const),
            pl.BlockSpec((TRI, TRI), const),
        ] + side_specs,
        out_specs=(pl.BlockSpec((IN_TM, IN_TN), lambda i, j: (i, j)),
                   pl.BlockSpec((IN_TM, GLA_QK), lambda i, j: (i, 0))) + tuple(side_specs),
        scratch_shapes=[pltpu.VMEM((IN_TM, D_MODEL), BF16)],
        compiler_params=pltpu.CompilerParams(
            dimension_semantics=("parallel", "arbitrary"), vmem_limit_bytes=VMEM_LIMIT),
        name="in_proj",
    )(x2, w_main, cos, sin, w_a, w_gate, bias, tri, *side_weights)


MIX_T = 4096
RET_BLOCK = 256


def _mixer_kernel(qr_ref, kr_ref, vr_ref, zr_ref, qg_ref, kg_ref, vg_ref, zg_ref, bg_ref,
                  dmat_ref, qdec_ref, kdec_ref, sdec_ref, gr_ref, gg_ref,
                  or_ref, og_ref, sr_ref, sg_ref):
    @pl.when(pl.program_id(2) == 0)
    def _():
        sr_ref[...] = jnp.zeros_like(sr_ref)
        sg_ref[...] = jnp.zeros_like(sg_ref)

    g_ret = gr_ref[...]
    g_gla = gg_ref[...]
    row = lax.broadcasted_iota(jnp.int32, (CHUNK, CHUNK), 0)
    col = lax.broadcasted_iota(jnp.int32, (CHUNK, CHUNK), 1)
    causal = row >= col

    def ret_block(r0):
        rows = slice(r0, r0 + RET_BLOCK)
        q = qr_ref[rows, :]
        k = kr_ref[rows, :]
        v = vr_ref[rows, :]
        p = (_dot_nt(q, k) * dmat_ref[0]).astype(BF16)
        s_old = sr_ref[...]
        o = _dot(p, v) + qdec_ref[0] * _dot(q, s_old.astype(BF16))
        vk = (v.astype(F32) * kdec_ref[0]).astype(BF16)
        sr_ref[...] = s_old * sdec_ref[0] + _dot_tn(k, vk)
        o = o - jnp.mean(o, axis=-1, keepdims=True)
        o = o * lax.rsqrt(jnp.mean(o * o, axis=-1, keepdims=True) + LN_EPS)
        or_ref[rows, :] = (o * g_ret * _silu(zr_ref[rows, :].astype(F32))).astype(or_ref.dtype)

    def gla_chunk(r0):
        rows = slice(r0, r0 + CHUNK)
        b = bg_ref[rows, :]
        b_mid = b[CHUNK // 2:CHUNK // 2 + 1, :]
        b_last = b[CHUNK - 1:CHUNK, :]
        bc = b - b_mid
        e_pos = jnp.exp(bc)
        e_neg = jnp.exp(-bc)
        q = qg_ref[rows, :].astype(F32)
        k = kg_ref[rows, :].astype(F32)
        v = vg_ref[rows, :]
        a_causal = _dot_nt((q * e_pos).astype(BF16), (k * e_neg).astype(BF16))
        a_anti = _dot_nt((q * e_neg).astype(BF16), (k * e_pos).astype(BF16))
        p = jnp.where(causal, a_causal, a_anti).astype(BF16)
        st_old = sg_ref[...]
        o = _dot(p, v) + _dot_nt((q * jnp.exp(b)).astype(BF16), st_old.astype(BF16))
        kb = (k * jnp.exp(b_last - b)).astype(BF16)
        sg_ref[...] = st_old * jnp.exp(b_last) + _dot_tn(v, kb)
        o = o * lax.rsqrt(jnp.mean(o * o, axis=-1, keepdims=True) + LN_EPS)
        og_ref[rows, :] = (o * g_gla * _silu(zg_ref[rows, :].astype(F32))).astype(og_ref.dtype)

    for r0 in range(0, MIX_T, RET_BLOCK):
        ret_block(r0)
        for c0 in range(r0, r0 + RET_BLOCK, CHUNK):
            gla_chunk(c0)


def _mixer(h, b_cum, dmat, qdec, kdec, sdec, g_ret, g_gla, batch, seq):
    m = h.shape[0]
    nt = seq // MIX_T
    rd = RET_DV
    c_kr = RET_QK // rd
    c_vr = 2 * RET_QK // rd
    c_zr = (2 * RET_QK + RET_V) // rd
    g0 = 2 * RET_QK + 2 * RET_V
    c_qg = g0 // GLA_DK
    c_kg = (g0 + GLA_QK) // GLA_DK
    c_vg = (g0 + 2 * GLA_QK) // rd
    c_zg = (g0 + 2 * GLA_QK + GLA_V) // rd

    def rows(bi, hi, ti):
        return bi * nt + ti

    def hspec(width, first):
        return pl.BlockSpec((MIX_T, width), lambda bi, hi, ti: (rows(bi, hi, ti), first + hi))

    def head_table(shape):
        return pl.BlockSpec((1,) + shape, lambda bi, hi, ti: (hi, 0, 0))

    out_spec = pl.BlockSpec((MIX_T, rd), lambda bi, hi, ti: (rows(bi, hi, ti), hi))
    return pl.pallas_call(
        _mixer_kernel,
        out_shape=(jax.ShapeDtypeStruct((m, RET_V), BF16), jax.ShapeDtypeStruct((m, GLA_V), BF16)),
        grid=(batch, RET_HEADS, nt),
        in_specs=[
            hspec(rd, 0), hspec(rd, c_kr), hspec(rd, c_vr), hspec(rd, c_zr),
            hspec(GLA_DK, c_qg), hspec(GLA_DK, c_kg), hspec(rd, c_vg), hspec(rd, c_zg),
            pl.BlockSpec((MIX_T, GLA_DK), lambda bi, hi, ti: (rows(bi, hi, ti), hi)),
            head_table((RET_BLOCK, RET_BLOCK)), head_table((RET_BLOCK, RET_DV)),
            head_table((RET_BLOCK, RET_DV)), head_table((1, RET_DV)),
            pl.BlockSpec((1, rd), lambda bi, hi, ti: (0, hi)),
            pl.BlockSpec((1, rd), lambda bi, hi, ti: (0, hi)),
        ],
        out_specs=(out_spec, out_spec),
        scratch_shapes=[pltpu.VMEM((RET_DK, RET_DV), F32), pltpu.VMEM((GLA_DV, GLA_DK), F32)],
        compiler_params=pltpu.CompilerParams(
            dimension_semantics=("parallel", "parallel", "arbitrary"), vmem_limit_bytes=VMEM_LIMIT),
        name="mixer",
    )(h, h, h, h, h, h, h, h, b_cum, dmat, qdec, kdec, sdec, g_ret, g_gla)


OUT_TM = 512
OUT_SUB = 128


def _out_proj_kernel(or_ref, og_ref, w_ref, x_ref, g_ref, b_ref, y_ref, yb_ref):
    for r in range(OUT_TM // OUT_SUB):
        rows = slice(r * OUT_SUB, (r + 1) * OUT_SUB)
        acc = _dot(or_ref[rows, :], w_ref[:RET_V, :]) + _dot(og_ref[rows, :], w_ref[RET_V:, :])
        y = _layer_norm(DEEPNORM_ALPHA * x_ref[rows, :] + acc, g_ref[...], b_ref[...])
        y_ref[rows, :] = y
        yb_ref[rows, :] = y.astype(yb_ref.dtype)


def _out_proj(o_r, o_g, w_out, x2, g, b):
    m = x2.shape[0]
    row = lambda i: (i, 0)
    const = lambda i: (0, 0)
    return pl.pallas_call(
        _out_proj_kernel,
        out_shape=(jax.ShapeDtypeStruct((m, D_MODEL), F32), jax.ShapeDtypeStruct((m, D_MODEL), BF16)),
        grid=(m // OUT_TM,),
        in_specs=[
            pl.BlockSpec((OUT_TM, RET_V), row),
            pl.BlockSpec((OUT_TM, GLA_V), row),
            pl.BlockSpec((RET_V + GLA_V, D_MODEL), const),
            pl.BlockSpec((OUT_TM, D_MODEL), row),
            pl.BlockSpec((1, D_MODEL), const),
            pl.BlockSpec((1, D_MODEL), const),
        ],
        out_specs=(pl.BlockSpec((OUT_TM, D_MODEL), row), pl.BlockSpec((OUT_TM, D_MODEL), row)),
        compiler_params=pltpu.CompilerParams(
            dimension_semantics=("parallel",), vmem_limit_bytes=VMEM_LIMIT),
        name="out_proj_ln",
    )(o_r, o_g, w_out, x2, g, b)


UP_TM = 1024
UP_TF = 512
HALO = 16


def _up_conv_kernel(x_ref, halo_ref, wv_ref, wg_ref, cv_ref, cg_ref, bv_ref, bg_ref,
                    o_ref, lhs_ref, uv_ref, ug_ref, *, tiles_per_seq):
    i = pl.program_id(0)
    j = pl.program_id(1)

    @pl.when(j == 0)
    def _():
        keep = (i % tiles_per_seq != 0).astype(halo_ref.dtype)
        lhs_ref[:HALO, :] = halo_ref[...] * keep
        lhs_ref[HALO:, :] = x_ref[...]

    def conv(u_ref, c_ref, bias_ref):
        u = u_ref[...]
        acc = c_ref[0:1, :] * u
        for tap in range(1, CONV_WIDTH):
            acc = c_ref[tap:tap + 1, :] * u + pltpu.roll(acc, 1, axis=0)
        return (acc + bias_ref[...])[HALO:, :]

    lhs = lhs_ref[...]
    ug_ref[...] = _dot(lhs, wg_ref[...])
    uv_ref[...] = _dot(lhs, wv_ref[...])
    gate = conv(ug_ref, cg_ref, bg_ref)
    val = conv(uv_ref, cv_ref, bv_ref)
    o_ref[...] = (_silu(gate) * val).astype(o_ref.dtype)


def _up_conv(x1b, w_up, w_conv, b_conv, seq):
    m = x1b.shape[0]
    nf = D_FF // UP_TF
    tiles_per_seq = seq // UP_TM
    halo_blocks = UP_TM // HALO
    kern = functools.partial(_up_conv_kernel, tiles_per_seq=tiles_per_seq)
    return pl.pallas_call(
        kern,
        out_shape=jax.ShapeDtypeStruct((m, D_FF), BF16),
        grid=(m // UP_TM, nf),
        in_specs=[
            pl.BlockSpec((UP_TM, D_MODEL), lambda i, j: (i, 0)),
            pl.BlockSpec((HALO, D_MODEL), lambda i, j: (jnp.maximum(i * halo_blocks - 1, 0), 0)),
            pl.BlockSpec((D_MODEL, UP_TF), lambda i, j: (0, j)),
            pl.BlockSpec((D_MODEL, UP_TF), lambda i, j: (0, nf + j)),
            pl.BlockSpec((CONV_WIDTH, UP_TF), lambda i, j: (0, j)),
            pl.BlockSpec((CONV_WIDTH, UP_TF), lambda i, j: (0, nf + j)),
            pl.BlockSpec((1, UP_TF), lambda i, j: (0, j)),
            pl.BlockSpec((1, UP_TF), lambda i, j: (0, nf + j)),
        ],
        out_specs=pl.BlockSpec((UP_TM, UP_TF), lambda i, j: (i, j)),
        scratch_shapes=[
            pltpu.VMEM((UP_TM + HALO, D_MODEL), BF16),
            pltpu.VMEM((UP_TM + HALO, UP_TF), F32),
            pltpu.VMEM((UP_TM + HALO, UP_TF), F32),
        ],
        compiler_params=pltpu.CompilerParams(
            dimension_semantics=("parallel", "arbitrary"), vmem_limit_bytes=VMEM_LIMIT),
        name="up_conv_gate",
    )(x1b, x1b, w_up, w_up, w_conv, w_conv, b_conv, b_conv)


DOWN_TM = 512
DOWN_SUB = 128


def _down_proj_kernel(a_ref, w_ref, x_ref, g_ref, b_ref, y_ref):
    for r in range(DOWN_TM // DOWN_SUB):
        rows = slice(r * DOWN_SUB, (r + 1) * DOWN_SUB)
        acc = _dot(a_ref[rows, :], w_ref[...])
        y_ref[rows, :] = _layer_norm(DEEPNORM_ALPHA * x_ref[rows, :] + acc, g_ref[...], b_ref[...])


def _down_proj(act, w_down, x1, g, b):
    m = x1.shape[0]
    row = lambda i: (i, 0)
    const = lambda i: (0, 0)
    return pl.pallas_call(
        _down_proj_kernel,
        out_shape=jax.ShapeDtypeStruct((m, D_MODEL), F32),
        grid=(m // DOWN_TM,),
        in_specs=[
            pl.BlockSpec((DOWN_TM, D_FF), row),
            pl.BlockSpec((D_FF, D_MODEL), const, pipeline_mode=pl.Buffered(1)),
            pl.BlockSpec((DOWN_TM, D_MODEL), row),
            pl.BlockSpec((1, D_MODEL), const),
            pl.BlockSpec((1, D_MODEL), const),
        ],
        out_specs=pl.BlockSpec((DOWN_TM, D_MODEL), row),
        compiler_params=pltpu.CompilerParams(
            dimension_semantics=("parallel",), vmem_limit_bytes=VMEM_LIMIT),
        name="down_proj_ln",
    )(act, w_down, x1, g, b)


def _rotary_tables(seq):
    half = RET_DK // 2
    inv_freq = ROPE_BASE ** (-np.arange(half, dtype=np.float64) / half)
    ang = np.arange(seq, dtype=np.float64)[:, None] * inv_freq[None, :]
    return jnp.asarray(np.cos(ang), F32), jnp.asarray(np.sin(ang), F32)


def _retention_tables():
    log_gamma = jnp.log1p(-jnp.exp2(-5.0 - jnp.arange(RET_HEADS, dtype=F32)))
    idx = jnp.arange(RET_BLOCK, dtype=F32)
    dist = idx[:, None] - idx[None, :]
    chunk_gap = idx[:, None] // CHUNK - idx[None, :] // CHUNK
    expo = jnp.where(chunk_gap == 0, jnp.abs(dist), dist)
    dmat = jnp.where(chunk_gap >= 0, jnp.exp(log_gamma[:, None, None] * expo), 0.0)
    qdec = jnp.exp(log_gamma[:, None] * (idx + 1.0))
    kdec = jnp.exp(log_gamma[:, None] * (RET_BLOCK - 1.0 - idx))
    sdec = jnp.exp(log_gamma * RET_BLOCK)
    qdec = jnp.broadcast_to(qdec[:, :, None], (RET_HEADS, RET_BLOCK, RET_DV))
    kdec = jnp.broadcast_to(kdec[:, :, None], (RET_HEADS, RET_BLOCK, RET_DV))
    sdec = jnp.broadcast_to(sdec[:, None, None], (RET_HEADS, 1, RET_DV))
    return dmat, qdec, kdec, sdec


def _chunk_tri(n):
    r = jnp.arange(n)
    same = (r[:, None] // CHUNK) == (r[None, :] // CHUNK)
    return (same & (r[:, None] >= r[None, :])).astype(BF16)


def kernel(x, w_in, w_gla_gate, b_gla_gate, g_ret, g_gla, w_out, ln1_g, ln1_b, w_up, w_conv,
           b_conv, w_down, ln2_g, ln2_b):
    batch, seq, d = x.shape
    m = batch * seq
    cos, sin = _rotary_tables(seq)
    dmat, qdec, kdec, sdec = _retention_tables()
    tri = _chunk_tri(TRI)
    x2 = x.reshape(m, d)
    for layer in range(DEPTH):
        w_in_l = w_in[layer]
        w_main = w_in_l.astype(BF16)
        w_a = jnp.pad(w_in_l[:, H_WIDTH:], ((0, 0), (0, LANES - GLA_GATE_RANK))).astype(BF16)
        w_gate = jnp.pad(w_gla_gate[layer], ((0, LANES - GLA_GATE_RANK), (0, 0))).astype(BF16)
        h, b_cum, w_up_b, w_down_b, w_out_b = _in_proj(
            x2, w_main, cos, sin, w_a, w_gate, b_gla_gate[layer].astype(F32)[None, :], tri,
            (w_up[layer], w_down[layer], w_out[layer]), seq)
        o_r, o_g = _mixer(h, b_cum, dmat, qdec, kdec, sdec,
                          g_ret[layer].astype(F32)[None, :], g_gla[layer].astype(F32)[None, :],
                          batch, seq)
        x1, x1b = _out_proj(o_r, o_g, w_out_b, x2, ln1_g[layer][None, :], ln1_b[layer][None, :])
        act = _up_conv(x1b, w_up_b, w_conv[layer], b_conv[layer][None, :], seq)
        x2 = _down_proj(act, w_down_b, x1, ln2_g[layer][None, :], ln2_b[layer][None, :])
    return x2.reshape(batch, seq, d)
```

```python
import functools

import jax
import jax.numpy as jnp
import numpy as np
from jax import lax
from jax.experimental import pallas as pl
from jax.experimental.pallas import tpu as pltpu

D_MODEL = 2048
CHUNK = 64
RET_HEADS = 4
RET_DK = 256
RET_DV = 256
GLA_HEADS = 4
GLA_DK = 128
GLA_DV = 256
GLA_GATE_RANK = 16
GLA_GATE_TAU = 16.0
ROPE_BASE = 10000.0
D_FF = 5632
CONV_WIDTH = 3
LN_EPS = 1e-5
DEPTH = 1
DEEPNORM_ALPHA = (2.0 * DEPTH) ** 0.25

RET_QK = RET_HEADS * RET_DK
RET_V = RET_HEADS * RET_DV
GLA_QK = GLA_HEADS * GLA_DK
GLA_V = GLA_HEADS * GLA_DV
H_WIDTH = 2 * RET_QK + 2 * RET_V + 2 * GLA_QK + 2 * GLA_V

LANES = 128
VMEM_LIMIT = 56 * 1024 * 1024

F32 = jnp.float32
BF16 = jnp.bfloat16


def _dot(a, b):
    return jnp.dot(a, b, preferred_element_type=F32)


def _dot_nt(a, b):
    return lax.dot_general(a, b, (((1,), (1,)), ((), ())), preferred_element_type=F32)


def _dot_tn(a, b):
    return lax.dot_general(a, b, (((0,), (0,)), ((), ())), preferred_element_type=F32)


def _silu(z):
    return z / (1.0 + jnp.exp(-z))


def _layer_norm(y, g, b):
    mu = jnp.mean(y, axis=-1, keepdims=True)
    yc = y - mu
    var = jnp.mean(yc * yc, axis=-1, keepdims=True)
    return yc * lax.rsqrt(var + LN_EPS) * g + b


IN_TM = 1024
IN_TN = 1024
IN_SUB = 256
TRI = IN_SUB


def _in_proj_kernel(x_ref, w_ref, cos_ref, sin_ref, wa_ref, wg_ref, bias_ref, tri_ref,
                    o_ref, b_ref, xb_ref):
    j = pl.program_id(1)

    def gate(xb, rows):
        half = D_MODEL // 2
        a = _dot(xb[:, :half], wa_ref[:half, :]) + _dot(xb[:, half:], wa_ref[half:, :])
        z = _dot(a.astype(BF16), wg_ref[...]) + bias_ref[...]
        log_a = (jnp.minimum(z, 0.0) - jnp.log1p(jnp.exp(-jnp.abs(z)))) * (1.0 / GLA_GATE_TAU)
        hi = log_a.astype(BF16)
        lo = (log_a - hi.astype(F32)).astype(BF16)
        b_ref[rows, :] = _dot(tri_ref[...], hi) + _dot(tri_ref[...], lo)

    def rotary(acc, rows, scale):
        cos = cos_ref[rows, :]
        sin = sin_ref[rows, :]
        for hd in range(IN_TN // RET_DK):
            lo = hd * RET_DK
            x1 = acc[:, lo:lo + LANES]
            x2 = acc[:, lo + LANES:lo + 2 * LANES]
            o_ref[rows, lo:lo + LANES] = ((x1 * cos - x2 * sin) * scale).astype(o_ref.dtype)
            o_ref[rows, lo + LANES:lo + 2 * LANES] = ((x1 * sin + x2 * cos) * scale).astype(o_ref.dtype)

    def gla_qk(acc, rows):
        o_ref[rows, :GLA_QK] = (acc[:, :GLA_QK] * (GLA_DK ** -0.5)).astype(o_ref.dtype)
        o_ref[rows, GLA_QK:] = acc[:, GLA_QK:].astype(o_ref.dtype)

    def plain(acc, rows):
        o_ref[rows, :] = acc.astype(o_ref.dtype)

    def project(epilogue):
        for r in range(IN_TM // IN_SUB):
            rows = slice(r * IN_SUB, (r + 1) * IN_SUB)
            epilogue(_dot(xb_ref[rows, :], w_ref[...]), rows)

    @pl.when(j == 0)
    def _():
        for r in range(IN_TM // IN_SUB):
            rows = slice(r * IN_SUB, (r + 1) * IN_SUB)
            xb = x_ref[rows, :].astype(xb_ref.dtype)
            xb_ref[rows, :] = xb
            rotary(_dot(xb, w_ref[...]), rows, 1.0)
            gate(xb, rows)

    @pl.when(j == 1)
    def _():
        project(functools.partial(rotary, scale=RET_DK ** -0.5))

    @pl.when(j == 4)
    def _():
        project(gla_qk)

    @pl.when((j == 2) | (j == 3) | (j >= 5))
    def _():
        project(plain)


def _in_proj(x2, w_main, cos, sin, w_a, w_gate, bias, tri, seq):
    m = x2.shape[0]
    nseq = seq // IN_TM
    const = lambda i, j: (0, 0)
    return pl.pallas_call(
        _in_proj_kernel,
        out_shape=(jax.ShapeDtypeStruct((m, H_WIDTH), BF16), jax.ShapeDtypeStruct((m, GLA_QK), F32)),
        grid=(m // IN_TM, H_WIDTH // IN_TN),
        in_specs=[
            pl.BlockSpec((IN_TM, D_MODEL), lambda i, j: (i, 0)),
            pl.BlockSpec((D_MODEL, IN_TN), lambda i, j: (0, j)),
            pl.BlockSpec((IN_TM, LANES), lambda i, j: (i % nseq, 0)),
            pl.BlockSpec((IN_TM, LANES), lambda i, j: (i % nseq, 0)),
            pl.BlockSpec((D_MODEL, LANES), const),
            pl.BlockSpec((LANES, GLA_QK), const),
            pl.BlockSpec((1, GLA_QK), const),
            pl.BlockSpec((TRI, TRI), const),
        ],
        out_specs=(pl.BlockSpec((IN_TM, IN_TN), lambda i, j: (i, j)),
                   pl.BlockSpec((IN_TM, GLA_QK), lambda i, j: (i, 0))),
        scratch_shapes=[pltpu.VMEM((IN_TM, D_MODEL), BF16)],
        compiler_params=pltpu.CompilerParams(
            dimension_semantics=("parallel", "arbitrary"), vmem_limit_bytes=VMEM_LIMIT),
        name="in_proj",
    )(x2, w_main, cos, sin, w_a, w_gate, bias, tri)


MIX_T = 4096
RET_BLOCK = 256


def _mixer_kernel(qr_ref, kr_ref, vr_ref, zr_ref, qg_ref, kg_ref, vg_ref, zg_ref, bg_ref,
                  dmat_ref, qdec_ref, kdec_ref, sdec_ref, gr_ref, gg_ref,
                  or_ref, og_ref, sr_ref, sg_ref):
    @pl.when(pl.program_id(2) == 0)
    def _():
        sr_ref[...] = jnp.zeros_like(sr_ref)
        sg_ref[...] = jnp.zeros_like(sg_ref)

    g_ret = gr_ref[...]
    g_gla = gg_ref[...]
    row = lax.broadcasted_iota(jnp.int32, (CHUNK, CHUNK), 0)
    col = lax.broadcasted_iota(jnp.int32, (CHUNK, CHUNK), 1)
    causal = row >= col

    def ret_block(r0):
        rows = slice(r0, r0 + RET_BLOCK)
        q = qr_ref[rows, :]
        k = kr_ref[rows, :]
        v = vr_ref[rows, :]
        p = (_dot_nt(q, k) * dmat_ref[0]).astype(BF16)
        s_old = sr_ref[...]
        o = _dot(p, v) + qdec_ref[0] * _dot(q, s_old.astype(BF16))
        vk = (v.astype(F32) * kdec_ref[0]).astype(BF16)
        sr_ref[...] = s_old * sdec_ref[0] + _dot_tn(k, vk)
        o = o - jnp.mean(o, axis=-1, keepdims=True)
        o = o * lax.rsqrt(jnp.mean(o * o, axis=-1, keepdims=True) + LN_EPS)
        or_ref[rows, :] = (o * g_ret * _silu(zr_ref[rows, :].astype(F32))).astype(or_ref.dtype)

    def gla_chunk(r0):
        rows = slice(r0, r0 + CHUNK)
        b = bg_ref[rows, :]
        b_mid = b[CHUNK // 2:CHUNK // 2 + 1, :]
        b_last = b[CHUNK - 1:CHUNK, :]
        bc = b - b_mid
        e_pos = jnp.exp(bc)
        e_neg = jnp.exp(-bc)
        q = qg_ref[rows, :].astype(F32)
        k = kg_ref[rows, :].astype(F32)
        v = vg_ref[rows, :]
        a_causal = _dot_nt((q * e_pos).astype(BF16), (k * e_neg).astype(BF16))
        a_anti = _dot_nt((q * e_neg).astype(BF16), (k * e_pos).astype(BF16))
        p = jnp.where(causal, a_causal, a_anti).astype(BF16)
        st_old = sg_ref[...]
        o = _dot(p, v) + _dot_nt((q * jnp.exp(b)).astype(BF16), st_old.astype(BF16))
        kb = (k * jnp.exp(b_last - b)).astype(BF16)
        sg_ref[...] = st_old * jnp.exp(b_last) + _dot_tn(v, kb)
        o = o * lax.rsqrt(jnp.mean(o * o, axis=-1, keepdims=True) + LN_EPS)
        og_ref[rows, :] = (o * g_gla * _silu(zg_ref[rows, :].astype(F32))).astype(og_ref.dtype)

    for r0 in range(0, MIX_T, RET_BLOCK):
        ret_block(r0)
        for c0 in range(r0, r0 + RET_BLOCK, CHUNK):
            gla_chunk(c0)


def _mixer(h, b_cum, dmat, qdec, kdec, sdec, g_ret, g_gla, batch, seq):
    m = h.shape[0]
    nt = seq // MIX_T
    rd = RET_DV
    c_kr = RET_QK // rd
    c_vr = 2 * RET_QK // rd
    c_zr = (2 * RET_QK + RET_V) // rd
    g0 = 2 * RET_QK + 2 * RET_V
    c_qg = g0 // GLA_DK
    c_kg = (g0 + GLA_QK) // GLA_DK
    c_vg = (g0 + 2 * GLA_QK) // rd
    c_zg = (g0 + 2 * GLA_QK + GLA_V) // rd

    def rows(bi, hi, ti):
        return bi * nt + ti

    def hspec(width, first):
        return pl.BlockSpec((MIX_T, width), lambda bi, hi, ti: (rows(bi, hi, ti), first + hi))

    def head_table(shape):
        return pl.BlockSpec((1,) + shape, lambda bi, hi, ti: (hi, 0, 0))

    out_spec = pl.BlockSpec((MIX_T, rd), lambda bi, hi, ti: (rows(bi, hi, ti), hi))
    return pl.pallas_call(
        _mixer_kernel,
        out_shape=(jax.ShapeDtypeStruct((m, RET_V), BF16), jax.ShapeDtypeStruct((m, GLA_V), BF16)),
        grid=(batch, RET_HEADS, nt),
        in_specs=[
            hspec(rd, 0), hspec(rd, c_kr), hspec(rd, c_vr), hspec(rd, c_zr),
            hspec(GLA_DK, c_qg), hspec(GLA_DK, c_kg), hspec(rd, c_vg), hspec(rd, c_zg),
            pl.BlockSpec((MIX_T, GLA_DK), lambda bi, hi, ti: (rows(bi, hi, ti), hi)),
            head_table((RET_BLOCK, RET_BLOCK)), head_table((RET_BLOCK, RET_DV)),
            head_table((RET_BLOCK, RET_DV)), head_table((1, RET_DV)),
            pl.BlockSpec((1, rd), lambda bi, hi, ti: (0, hi)),
            pl.BlockSpec((1, rd), lambda bi, hi, ti: (0, hi)),
        ],
        out_specs=(out_spec, out_spec),
        scratch_shapes=[pltpu.VMEM((RET_DK, RET_DV), F32), pltpu.VMEM((GLA_DV, GLA_DK), F32)],
        compiler_params=pltpu.CompilerParams(
            dimension_semantics=("parallel", "parallel", "arbitrary"), vmem_limit_bytes=VMEM_LIMIT),
        name="mixer",
    )(h, h, h, h, h, h, h, h, b_cum, dmat, qdec, kdec, sdec, g_ret, g_gla)


OUT_TM = 512
OUT_SUB = 128


def _out_proj_kernel(or_ref, og_ref, w_ref, x_ref, g_ref, b_ref, y_ref, yb_ref):
    for r in range(OUT_TM // OUT_SUB):
        rows = slice(r * OUT_SUB, (r + 1) * OUT_SUB)
        acc = _dot(or_ref[rows, :], w_ref[:RET_V, :]) + _dot(og_ref[rows, :], w_ref[RET_V:, :])
        y = _layer_norm(DEEPNORM_ALPHA * x_ref[rows, :] + acc, g_ref[...], b_ref[...])
        y_ref[rows, :] = y
        yb_ref[rows, :] = y.astype(yb_ref.dtype)


def _out_proj(o_r, o_g, w_out, x2, g, b):
    m = x2.shape[0]
    row = lambda i: (i, 0)
    const = lambda i: (0, 0)
    return pl.pallas_call(
        _out_proj_kernel,
        out_shape=(jax.ShapeDtypeStruct((m, D_MODEL), F32), jax.ShapeDtypeStruct((m, D_MODEL), BF16)),
        grid=(m // OUT_TM,),
        in_specs=[
            pl.BlockSpec((OUT_TM, RET_V), row),
            pl.BlockSpec((OUT_TM, GLA_V), row),
            pl.BlockSpec((RET_V + GLA_V, D_MODEL), const),
            pl.BlockSpec((OUT_TM, D_MODEL), row),
            pl.BlockSpec((1, D_MODEL), const),
            pl.BlockSpec((1, D_MODEL), const),
        ],
        out_specs=(pl.BlockSpec((OUT_TM, D_MODEL), row), pl.BlockSpec((OUT_TM, D_MODEL), row)),
        compiler_params=pltpu.CompilerParams(
            dimension_semantics=("parallel",), vmem_limit_bytes=VMEM_LIMIT),
        name="out_proj_ln",
    )(o_r, o_g, w_out, x2, g, b)


UP_TM = 1024
UP_TF = 512
UP_EP = 256
HALO = 16


def _up_conv_kernel(x_ref, halo_ref, wv_ref, wg_ref, cv_ref, cg_ref, bv_ref, bg_ref,
                    o_ref, lhs_ref, uv_ref, ug_ref, *, tiles_per_seq):
    i = pl.program_id(0)
    j = pl.program_id(1)

    @pl.when(j == 0)
    def _():
        keep = (i % tiles_per_seq != 0).astype(halo_ref.dtype)
        lhs_ref[:HALO, :] = halo_ref[...] * keep
        lhs_ref[HALO:, :] = x_ref[...]

    def conv(u_ref, r0, c_ref, bias_ref):
        u = u_ref[r0:r0 + UP_EP + HALO, :]
        acc = c_ref[0:1, :] * u
        for tap in range(1, CONV_WIDTH):
            acc = c_ref[tap:tap + 1, :] * u + pltpu.roll(acc, 1, axis=0)
        return (acc + bias_ref[...])[HALO:, :]

    lhs = lhs_ref[...]
    ug_ref[...] = _dot(lhs, wg_ref[...])
    uv_ref[...] = _dot(lhs, wv_ref[...])
    for r0 in range(0, UP_TM, UP_EP):
        gate = conv(ug_ref, r0, cg_ref, bg_ref)
        val = conv(uv_ref, r0, cv_ref, bv_ref)
        o_ref[r0:r0 + UP_EP, :] = (_silu(gate) * val).astype(o_ref.dtype)


def _up_conv(x1b, w_up, w_conv, b_conv, seq):
    m = x1b.shape[0]
    nf = D_FF // UP_TF
    tiles_per_seq = seq // UP_TM
    halo_blocks = UP_TM // HALO
    kern = functools.partial(_up_conv_kernel, tiles_per_seq=tiles_per_seq)
    return pl.pallas_call(
        kern,
        out_shape=jax.ShapeDtypeStruct((m, D_FF), BF16),
        grid=(m // UP_TM, nf),
        in_specs=[
            pl.BlockSpec((UP_TM, D_MODEL), lambda i, j: (i, 0)),
            pl.BlockSpec((HALO, D_MODEL), lambda i, j: (jnp.maximum(i * halo_blocks - 1, 0), 0)),
            pl.BlockSpec((D_MODEL, UP_TF), lambda i, j: (0, j)),
            pl.BlockSpec((D_MODEL, UP_TF), lambda i, j: (0, nf + j)),
            pl.BlockSpec((CONV_WIDTH, UP_TF), lambda i, j: (0, j)),
            pl.BlockSpec((CONV_WIDTH, UP_TF), lambda i, j: (0, nf + j)),
            pl.BlockSpec((1, UP_TF), lambda i, j: (0, j)),
            pl.BlockSpec((1, UP_TF), lambda i, j: (0, nf + j)),
        ],
        out_specs=pl.BlockSpec((UP_TM, UP_TF), lambda i, j: (i, j)),
        scratch_shapes=[
            pltpu.VMEM((UP_TM + HALO, D_MODEL), BF16),
            pltpu.VMEM((UP_TM + HALO, UP_TF), F32),
            pltpu.VMEM((UP_TM + HALO, UP_TF), F32),
        ],
        compiler_params=pltpu.CompilerParams(
            dimension_semantics=("parallel", "arbitrary"), vmem_limit_bytes=VMEM_LIMIT),
        name="up_conv_gate",
    )(x1b, x1b, w_up, w_up, w_conv, w_conv, b_conv, b_conv)


DOWN_TM = 512
DOWN_SUB = 128


def _down_proj_kernel(a_ref, w_ref, x_ref, g_ref, b_ref, y_ref):
    for r in range(DOWN_TM // DOWN_SUB):
        rows = slice(r * DOWN_SUB, (r + 1) * DOWN_SUB)
        acc = _dot(a_ref[rows, :], w_ref[...])
        y_ref[rows, :] = _layer_norm(DEEPNORM_ALPHA * x_ref[rows, :] + acc, g_ref[...], b_ref[...])


def _down_proj(act, w_down, x1, g, b):
    m = x1.shape[0]
    row = lambda i: (i, 0)
    const = lambda i: (0, 0)
    return pl.pallas_call(
        _down_proj_kernel,
        out_shape=jax.ShapeDtypeStruct((m, D_MODEL), F32),
        grid=(m // DOWN_TM,),
        in_specs=[
            pl.BlockSpec((DOWN_TM, D_FF), row),
            pl.BlockSpec((D_FF, D_MODEL), const, pipeline_mode=pl.Buffered(1)),
            pl.BlockSpec((DOWN_TM, D_MODEL), row),
            pl.BlockSpec((1, D_MODEL), const),
            pl.BlockSpec((1, D_MODEL), const),
        ],
        out_specs=pl.BlockSpec((DOWN_TM, D_MODEL), row),
        compiler_params=pltpu.CompilerParams(
            dimension_semantics=("parallel",), vmem_limit_bytes=VMEM_LIMIT),
        name="down_proj_ln",
    )(act, w_down, x1, g, b)


def _rotary_tables(seq):
    half = RET_DK // 2
    inv_freq = ROPE_BASE ** (-np.arange(half, dtype=np.float64) / half)
    ang = np.arange(seq, dtype=np.float64)[:, None] * inv_freq[None, :]
    return jnp.asarray(np.cos(ang), F32), jnp.asarray(np.sin(ang), F32)


def _retention_tables():
    log_gamma = jnp.log1p(-jnp.exp2(-5.0 - jnp.arange(RET_HEADS, dtype=F32)))
    idx = jnp.arange(RET_BLOCK, dtype=F32)
    dist = idx[:, None] - idx[None, :]
    chunk_gap = idx[:, None] // CHUNK - idx[None, :] // CHUNK
    expo = jnp.where(chunk_gap == 0, jnp.abs(dist), dist)
    dmat = jnp.where(chunk_gap >= 0, jnp.exp(log_gamma[:, None, None] * expo), 0.0)
    qdec = jnp.exp(log_gamma[:, None] * (idx + 1.0))
    kdec = jnp.exp(log_gamma[:, None] * (RET_BLOCK - 1.0 - idx))
    sdec = jnp.exp(log_gamma * RET_BLOCK)
    qdec = jnp.broadcast_to(qdec[:, :, None], (RET_HEADS, RET_BLOCK, RET_DV))
    kdec = jnp.broadcast_to(kdec[:, :, None], (RET_HEADS, RET_BLOCK, RET_DV))
    sdec = jnp.broadcast_to(sdec[:, None, None], (RET_HEADS, 1, RET_DV))
    return dmat, qdec, kdec, sdec


def _chunk_tri(n):
    r = jnp.arange(n)
    same = (r[:, None] // CHUNK) == (r[None, :] // CHUNK)
    return (same & (r[:, None] >= r[None, :])).astype(BF16)


def kernel(x, w_in, w_gla_gate, b_gla_gate, g_ret, g_gla, w_out, ln1_g, ln1_b, w_up, w_conv,
           b_conv, w_down, ln2_g, ln2_b):
    batch, seq, d = x.shape
    m = batch * seq
    cos, sin = _rotary_tables(seq)
    dmat, qdec, kdec, sdec = _retention_tables()
    tri = _chunk_tri(TRI)
    x2 = x.reshape(m, d)
    for layer in range(DEPTH):
        w_in_l = w_in[layer]
        w_main = w_in_l.astype(BF16)
        w_a = jnp.pad(w_in_l[:, H_WIDTH:], ((0, 0), (0, LANES - GLA_GATE_RANK))).astype(BF16)
        w_gate = jnp.pad(w_gla_gate[layer], ((0, LANES - GLA_GATE_RANK), (0, 0))).astype(BF16)
        h, b_cum = _in_proj(x2, w_main, cos, sin, w_a, w_gate,
                            b_gla_gate[layer].astype(F32)[None, :], tri, seq)
        o_r, o_g = _mixer(h, b_cum, dmat, qdec, kdec, sdec,
                          g_ret[layer].astype(F32)[None, :], g_gla[layer].astype(F32)[None, :],
                          batch, seq)
        x1, x1b = _out_proj(o_r, o_g, w_out[layer].astype(BF16), x2,
                            ln1_g[layer][None, :], ln1_b[layer][None, :])
        act = _up_conv(x1b, w_up[layer].astype(BF16), w_conv[layer], b_conv[layer][None, :], seq)
        x2 = _down_proj(act, w_down[layer].astype(BF16), x1,
                        ln2_g[layer][None, :], ln2_b[layer][None, :])
    return x2.reshape(batch, seq, d)
```
